```python
import jax, jax.numpy as jnp
from jax import lax
import numpy as np

D_MODEL = 1024
BATCH = 8
SEQ = 2048
DEPTH = 4
DEC_BATCH = 128
DEC_SEQ = 4
PAST_LEN = 16384
PAGE_SIZE = 128

N_META = 16
MIX_WIDTH = D_MODEL
HG_WIDTH = MIX_WIDTH // 2
HG_HEADS = 4
HG_DK = HG_WIDTH // HG_HEADS
HG_DV = HG_WIDTH // HG_HEADS
POOL_WIDTH = MIX_WIDTH - HG_WIDTH
POOL_WINDOWS = (2, 4, 8, 16)
POOL_GROUPS = len(POOL_WINDOWS)
POOL_GC = POOL_WIDTH // POOL_GROUPS
POOL_STATE = max(POOL_WINDOWS) - 1
IN_COLS = 4 * HG_WIDTH + POOL_WIDTH
D_FF = ((8 * D_MODEL // 3 + 127) // 128) * 128
CHUNK = 64
EPS = 1e-6

kernel_name = "hgrn2_pool_macaron_hybrid_step"


def rmsnorm(x, gain):
    xf = x.astype(jnp.float32)
    y = xf * lax.rsqrt(jnp.mean(xf * xf, axis=-1, keepdims=True) + EPS)
    return (y * gain.astype(jnp.float32)).astype(x.dtype)


def swiglu(h, w_in, w_out):
    gate, up = jnp.split(h @ w_in, 2, axis=-1)
    return (jax.nn.silu(gate) * up) @ w_out


def layer_lower_bounds(lb_logits):
    p = jax.nn.softmax(lb_logits.astype(jnp.float32), axis=0)
    cs = jnp.cumsum(p, axis=0)
    return cs - cs[0:1]


def hgrn_chunk(S, q, k, v, g):
    L = q.shape[2]
    b = jnp.cumsum(g, axis=2)
    causal = jnp.tril(jnp.ones((L, L), dtype=bool))
    diff = b[:, :, :, None, :] - b[:, :, None, :, :]
    decay = jnp.exp(jnp.where(causal[None, None, :, :, None], diff, -jnp.inf))
    scores = jnp.einsum('bhtd,bhtsd,bhsd->bhts', q, decay, k)
    o = jnp.einsum('bhts,bhsv->bhtv', scores, v) + jnp.einsum('bhtd,bhdv->bhtv', q * jnp.exp(b), S)
    b_last = b[:, :, -1:, :]
    S_new = jnp.exp(b_last[:, :, 0, :])[..., None] * S + jnp.einsum('bhsd,bhsv->bhdv', k * jnp.exp(b_last - b), v)
    return S_new, o


def hgrn_prompt(q, k, v, g):
    Bn, H, T, _ = q.shape
    S0 = jnp.zeros((Bn, H, HG_DK, HG_DV), jnp.float32)
    S1, o_meta = hgrn_chunk(S0, q[:, :, :N_META], k[:, :, :N_META], v[:, :, :N_META], g[:, :, :N_META])
    n_chunks = (T - N_META) // CHUNK

    def to_chunks(a):
        a = a[:, :, N_META:].reshape(Bn, H, n_chunks, CHUNK, a.shape[-1])
        return jnp.moveaxis(a, 2, 0)

    S_fin, o_rest = lax.scan(lambda S, xs: hgrn_chunk(S, *xs), S1,
                             (to_chunks(q), to_chunks(k), to_chunks(v), to_chunks(g)))
    o_rest = jnp.moveaxis(o_rest, 0, 2).reshape(Bn, H, T - N_META, HG_DV)
    return S_fin, jnp.concatenate([o_meta, o_rest], axis=2)


def multiscale_pool(u_ext, n_out, pool_w, pool_scale):
    Bn, T, C = u_ext.shape
    uf = u_ext.astype(jnp.float32)
    cs0 = jnp.concatenate([jnp.zeros((Bn, 1, C), jnp.float32), jnp.cumsum(uf, axis=1)], axis=1)
    idx = jnp.arange(1, T + 1, dtype=jnp.float32)
    means = []
    for gi, w in enumerate(POOL_WINDOWS):
        c = cs0[:, :, gi * POOL_GC:(gi + 1) * POOL_GC]
        shifted = jnp.concatenate([jnp.zeros((Bn, w, POOL_GC), jnp.float32), c[:, :T + 1 - w]], axis=1)
        count = jnp.minimum(jnp.float32(w), idx)
        means.append((c[:, 1:] - shifted[:, 1:]) / count[None, :, None])
    pooled = jnp.concatenate(means, axis=-1)[:, T - n_out:]
    d = (pooled - uf[:, T - n_out:]).reshape(Bn, n_out, POOL_GROUPS, POOL_GC)
    y = jnp.einsum('btgc,gcd->btgd', d, pool_w.astype(jnp.float32)).reshape(Bn, n_out, POOL_WIDTH)
    return y * pool_scale.astype(jnp.float32)


def token_mixing(h, lb, w_in, hg_norm, pool_w, pool_scale, w_out, hg_state, pool_prev):
    Bn, T, _ = h.shape
    z = h @ w_in
    zq = z[..., :HG_WIDTH].astype(jnp.float32)
    zf = z[..., HG_WIDTH:2 * HG_WIDTH].astype(jnp.float32)
    zi = z[..., 2 * HG_WIDTH:3 * HG_WIDTH].astype(jnp.float32)
    zg = z[..., 3 * HG_WIDTH:4 * HG_WIDTH].astype(jnp.float32)
    zp = z[..., 4 * HG_WIDTH:]

    def heads(a):
        return a.reshape(Bn, T, HG_HEADS, -1).transpose(0, 2, 1, 3)

    q = heads(jax.nn.silu(zq))
    k = heads((1.0 - lb) * jax.nn.sigmoid(-zf))
    g = heads(jnp.logaddexp(jnp.log(lb), jnp.log1p(-lb) + jax.nn.log_sigmoid(zf)))
    v = heads(zi)
    if hg_state is None:
        S_new, o = hgrn_prompt(q, k, v, g)
    else:
        S_new, o = hgrn_chunk(hg_state.astype(jnp.float32), q, k, v, g)
    o = o * lax.rsqrt(jnp.mean(o * o, axis=-1, keepdims=True) + EPS) * hg_norm.astype(jnp.float32)[None, :, None, :]
    o = o.transpose(0, 2, 1, 3).reshape(Bn, T, HG_WIDTH) * jax.nn.silu(zg)

    u_ext = zp if pool_prev is None else jnp.concatenate([pool_prev.astype(zp.dtype), zp], axis=1)
    p = multiscale_pool(u_ext, T, pool_w, pool_scale)
    mixed = jnp.concatenate([o.astype(h.dtype), p.astype(h.dtype)], axis=-1) @ w_out
    return mixed, S_new, u_ext[:, -POOL_STATE:]


def run_trunk(x, state_hgrn, state_pool, lbs, norm_ffn1, w_ffn1_in, w_ffn1_out, norm_mix, w_in, hg_norm,
              pool_w, pool_scale, w_out, norm_ffn2, w_ffn2_in, w_ffn2_out, norm_final):
    hg_states, pool_states = [], []
    for l in range(DEPTH):
        x = x + 0.5 * swiglu(rmsnorm(x, norm_ffn1[l]), w_ffn1_in[l], w_ffn1_out[l])
        m, S, u = token_mixing(rmsnorm(x, norm_mix[l]), lbs[l], w_in[l], hg_norm[l], pool_w[l], pool_scale[l],
                               w_out[l],
                               None if state_hgrn is None else state_hgrn[l],
                               None if state_pool is None else state_pool[l])
        x = x + m
        x = x + 0.5 * swiglu(rmsnorm(x, norm_ffn2[l]), w_ffn2_in[l], w_ffn2_out[l])
        hg_states.append(S)
        pool_states.append(u)
    return rmsnorm(x, norm_final), jnp.stack(hg_states), jnp.stack(pool_states)


def setup_inputs(seed: int = 0) -> dict:
    key = jax.random.key(seed)
    ks = jax.random.split(key, 20)
    f32 = jnp.float32
    nrm = lambda k, shape, scale: jax.random.normal(k, shape, f32) * scale
    return {
        'x_prompt': nrm(ks[0], (BATCH, SEQ, D_MODEL), 1.0),
        'x_sample': nrm(ks[1], (DEC_BATCH, DEC_SEQ, D_MODEL), 1.0),
        'state_hgrn': nrm(ks[2], (DEPTH, DEC_BATCH, HG_HEADS, HG_DK, HG_DV), 0.5),
        'state_pool': nrm(ks[3], (DEPTH, DEC_BATCH, POOL_STATE, POOL_WIDTH), 1.0),
        'meta': nrm(ks[4], (N_META, D_MODEL), 1.0),
        'lb_logits': nrm(ks[5], (DEPTH, HG_WIDTH), 0.5),
        'norm_ffn1': 1.0 + nrm(ks[6], (DEPTH, D_MODEL), 0.05),
        'w_ffn1_in': nrm(ks[7], (DEPTH, D_MODEL, 2 * D_FF), D_MODEL ** -0.5),
        'w_ffn1_out': nrm(ks[8], (DEPTH, D_FF, D_MODEL), D_FF ** -0.5),
        'norm_mix': 1.0 + nrm(ks[9], (DEPTH, D_MODEL), 0.05),
        'w_in': nrm(ks[10], (DEPTH, D_MODEL, IN_COLS), D_MODEL ** -0.5),
        'hg_norm': 1.0 + nrm(ks[11], (DEPTH, HG_HEADS, HG_DV), 0.05),
        'pool_w': nrm(ks[12], (DEPTH, POOL_GROUPS, POOL_GC, POOL_GC), POOL_GC ** -0.5),
        'pool_scale': 1.0 + nrm(ks[13], (DEPTH, POOL_WIDTH), 0.1),
        'w_out': nrm(ks[14], (DEPTH, MIX_WIDTH, D_MODEL), MIX_WIDTH ** -0.5),
        'norm_ffn2': 1.0 + nrm(ks[15], (DEPTH, D_MODEL), 0.05),
        'w_ffn2_in': nrm(ks[16], (DEPTH, D_MODEL, 2 * D_FF), D_MODEL ** -0.5),
        'w_ffn2_out': nrm(ks[17], (DEPTH, D_FF, D_MODEL), D_FF ** -0.5),
        'norm_final': 1.0 + nrm(ks[18], (D_MODEL,), 0.05),
    }


def reference(x_prompt, x_sample, state_hgrn, state_pool, meta, lb_logits, norm_ffn1, w_ffn1_in, w_ffn1_out,
              norm_mix, w_in, hg_norm, pool_w, pool_scale, w_out, norm_ffn2, w_ffn2_in, w_ffn2_out, norm_final):
    lbs = layer_lower_bounds(lb_logits)
    weights = (norm_ffn1, w_ffn1_in, w_ffn1_out, norm_mix, w_in, hg_norm, pool_w, pool_scale, w_out,
               norm_ffn2, w_ffn2_in, w_ffn2_out, norm_final)
    meta_b = jnp.broadcast_to(meta.astype(x_prompt.dtype)[None], (x_prompt.shape[0], N_META, D_MODEL))
    xp = jnp.concatenate([meta_b, x_prompt], axis=1)
    yp, hg_p, pool_p = run_trunk(xp, None, None, lbs, *weights)
    y_prompt = yp[:, N_META:]
    y_sample, hg_s, pool_s = run_trunk(x_sample, state_hgrn, state_pool, lbs, *weights)
    return (y_prompt, y_sample, hg_p, pool_p, hg_s, pool_s)
```

```python
import functools

import jax
import jax.numpy as jnp
from jax import lax
from jax.experimental import pallas as pl
from jax.experimental.pallas import tpu as pltpu

F32 = jnp.float32
BF16 = jnp.bfloat16

D_MODEL = 1024
DEPTH = 4
N_META = 16
HG_WIDTH = 512
HG_HEADS = 4
HG_D = 128
POOL_WIDTH = 512
POOL_WINDOWS = (2, 4, 8, 16)
POOL_GC = 128
POOL_STATE = 15
IN_COLS = 4 * HG_WIDTH + POOL_WIDTH
D_FF = 2816
EPS = 1e-6

SUBLANES = 8
MXU_DIM = 256
FF_CHUNK = MXU_DIM
ROW_TILE = 256
VMEM_LIMIT = 56 * 1024 * 1024
NEG_BIG = -1e30


def _const_spec(shape):
    nd = len(shape)
    return pl.BlockSpec(shape, lambda *_: (0,) * nd, pipeline_mode=pl.Buffered(1))


def _rmsnorm(x, gain):
    return x * lax.rsqrt(jnp.mean(x * x, axis=-1, keepdims=True) + EPS) * gain


def _dot(a, b):
    return jnp.dot(a, b, preferred_element_type=F32)


def _ffn_half_step(x, gain, win_ref, wout_ref, acc_ref):
    hn = _rmsnorm(x, gain).astype(BF16)
    for c in range(D_FF // FF_CHUNK):
        lo = c * FF_CHUNK
        gate = _dot(hn, win_ref[:, lo:lo + FF_CHUNK])
        up = _dot(hn, win_ref[:, D_FF + lo:D_FF + lo + FF_CHUNK])
        act = (gate * (1.0 / (1.0 + jnp.exp(-gate))) * up).astype(BF16)
        part = _dot(act, wout_ref[lo:lo + FF_CHUNK, :])
        if c == 0:
            acc_ref[...] = part
        else:
            acc_ref[...] += part
    return x + 0.5 * acc_ref[...]


def _pre_kernel(x_ref, g1_ref, w1i_ref, w1o_ref, gm_ref, win_ref, lbp_ref,
                x1_ref, q_ref, k_ref, lf_ref, v_ref, gt_ref, zp_ref, acc_ref):
    x1 = _ffn_half_step(x_ref[...], g1_ref[...], w1i_ref, w1o_ref, acc_ref)
    x1_ref[...] = x1
    hn = _rmsnorm(x1, gm_ref[...]).astype(BF16)
    W = HG_WIDTH
    zq = _dot(hn, win_ref[:, 0:W])
    q_ref[...] = zq * (1.0 / (1.0 + jnp.exp(-zq)))
    zf = _dot(hn, win_ref[:, W:2 * W])
    one_m_lb = lbp_ref[0:1, :]
    log_lb = lbp_ref[1:2, :]
    log1m_lb = lbp_ref[2:3, :]
    e = jnp.exp(-jnp.abs(zf))
    r = 1.0 / (1.0 + e)
    k_ref[...] = one_m_lb * (jnp.where(zf >= 0.0, e, 1.0) * r)
    c = log1m_lb + (jnp.minimum(zf, 0.0) - jnp.log1p(e))
    lf_ref[...] = jnp.maximum(log_lb, c) + jnp.log1p(jnp.exp(-jnp.abs(log_lb - c)))
    v_ref[...] = _dot(hn, win_ref[:, 2 * W:3 * W])
    zg = _dot(hn, win_ref[:, 3 * W:4 * W])
    gt_ref[...] = zg * (1.0 / (1.0 + jnp.exp(-zg)))
    zp_ref[...] = _dot(hn, win_ref[:, 4 * W:5 * W])


def _pre_call(x_rows, g1, w1i, w1o, gm, win, lbp):
    m = x_rows.shape[0]
    tm = ROW_TILE
    row = lambda w: pl.BlockSpec((tm, w), lambda i: (i, 0))
    outs = [jax.ShapeDtypeStruct((m, D_MODEL), F32)] + [jax.ShapeDtypeStruct((m, HG_WIDTH), F32)] * 6
    return pl.pallas_call(
        _pre_kernel,
        grid=(m // tm,),
        in_specs=[row(D_MODEL), _const_spec((1, D_MODEL)), _const_spec((D_MODEL, 2 * D_FF)),
                  _const_spec((D_FF, D_MODEL)), _const_spec((1, D_MODEL)),
                  _const_spec((D_MODEL, IN_COLS)), _const_spec((SUBLANES, HG_WIDTH))],
        out_specs=[row(D_MODEL)] + [row(HG_WIDTH)] * 6,
        out_shape=outs,
        scratch_shapes=[pltpu.VMEM((tm, D_MODEL), F32)],
        compiler_params=pltpu.CompilerParams(dimension_semantics=("arbitrary",),
                                             vmem_limit_bytes=VMEM_LIMIT),
        name="pre",
    )(x_rows, g1, w1i, w1o, gm, win, lbp)


def _post_kernel(x1_ref, mix_ref, wo_ref, g2_ref, w2i_ref, w2o_ref, gf_ref, y_ref, acc_ref, *, final):
    x2 = x1_ref[...] + _dot(mix_ref[...], wo_ref[...])
    x3 = _ffn_half_step(x2, g2_ref[...], w2i_ref, w2o_ref, acc_ref)
    y_ref[...] = _rmsnorm(x3, gf_ref[...]) if final else x3


def _post_call(x1_rows, mix_rows, wo, g2, w2i, w2o, gf, final):
    m = x1_rows.shape[0]
    tm = ROW_TILE
    row = lambda w: pl.BlockSpec((tm, w), lambda i: (i, 0))
    return pl.pallas_call(
        functools.partial(_post_kernel, final=final),
        grid=(m // tm,),
        in_specs=[row(D_MODEL), row(D_MODEL), _const_spec((D_MODEL, D_MODEL)), _const_spec((1, D_MODEL)),
                  _const_spec((D_MODEL, 2 * D_FF)), _const_spec((D_FF, D_MODEL)), _const_spec((1, D_MODEL))],
        out_specs=row(D_MODEL),
        out_shape=jax.ShapeDtypeStruct((m, D_MODEL), F32),
        scratch_shapes=[pltpu.VMEM((tm, D_MODEL), F32)],
        compiler_params=pltpu.CompilerParams(dimension_semantics=("arbitrary",),
                                             vmem_limit_bytes=VMEM_LIMIT),
        name="post",
    )(x1_rows, mix_rows, wo, g2, w2i, w2o, gf)


def _intra_scores(bh, qh, kh, L):
    D = HG_D
    row_l = lax.broadcasted_iota(jnp.int32, (L, D), 0)
    row_i = lax.broadcasted_iota(jnp.int32, (L, L), 0)
    col_i = lax.broadcasted_iota(jnp.int32, (L, L), 1)
    scores = jnp.zeros((L, L), F32)
    half = L // 2
    while half >= SUBLANES:
        pair = 2 * half
        pieces = [jnp.broadcast_to(bh[p * pair + half - 1:p * pair + half, :], (pair, D))
                  for p in range(L // pair)]
        bref = pieces[0] if len(pieces) == 1 else jnp.concatenate(pieces, axis=0)
        second = (row_l & half) != 0
        qt = (qh * jnp.exp(jnp.where(second, bh - bref, NEG_BIG))).astype(BF16)
        kt = (kh * jnp.exp(jnp.where(second, NEG_BIG, bref - bh))).astype(BF16)
        blk = lax.dot_general(qt, kt, (((1,), (1,)), ((), ())), preferred_element_type=F32)
        same_pair = (row_i & -pair) == (col_i & -pair)
        scores = scores + jnp.where(same_pair, blk, 0.0)
        half //= 2

    nb = L // SUBLANES
    b3 = bh.reshape(nb, SUBLANES, D)
    q3 = qh.reshape(nb, SUBLANES, D)
    k3 = kh.reshape(nb, SUBLANES, D)
    t_in = lax.broadcasted_iota(jnp.int32, (nb, SUBLANES, D), 1)
    lane = lax.broadcasted_iota(jnp.int32, (nb, SUBLANES, L), 2)
    blk0 = lax.broadcasted_iota(jnp.int32, (nb, SUBLANES, L), 0) * SUBLANES
    diag = jnp.zeros((nb, SUBLANES, L), F32)
    for s in range(SUBLANES):
        bs = b3[:, s:s + 1, :]
        ks = k3[:, s:s + 1, :]
        p = q3 * ks * jnp.exp(jnp.where(t_in >= s, b3 - bs, NEG_BIG))
        col = jnp.sum(p, axis=-1, keepdims=True)
        diag = jnp.where(lane == blk0 + s, col, diag)
    return scores + diag.reshape(L, L)


def _time_cumsum(lf, L):
    if L == SUBLANES:
        rows = [lf[0:1, :]]
        for r in range(1, L):
            rows.append(rows[-1] + lf[r:r + 1, :])
        return jnp.concatenate(rows, axis=0)
    tri = (lax.broadcasted_iota(jnp.int32, (L, L), 0) >= lax.broadcasted_iota(jnp.int32, (L, L), 1)).astype(BF16)
    l1 = lf.astype(BF16)
    r1 = lf - l1.astype(F32)
    l2 = r1.astype(BF16)
    l3 = (r1 - l2.astype(F32)).astype(BF16)
    return _dot(tri, l1) + _dot(tri, l2) + _dot(tri, l3)


def _mix_kernel(q_ref, k_ref, lf_ref, v_ref, gt_ref, zp_ref, s0_ref, pp_ref, cnt_ref, hgn_ref, pw_ref, ps_ref,
                *rest, tt, chunk, nseq, n_tiles, t_real, shared_state):
    mix_ref, sout_ref, pout_ref, st_scr, ubuf, mixf = rest[-6:]
    D = HG_D
    L = chunk
    ti = pl.program_id(1)

    def load_state(n):
        sidx = 0 if shared_state else n
        for h in range(HG_HEADS):
            st_scr[h] = s0_ref[sidx, h].T
        ubuf[1:1 + POOL_STATE, :] = pp_ref[sidx]

    def chunk_step(c, base):
        r0 = pl.multiple_of(base + c * L, SUBLANES)
        rows = pl.ds(r0, L)
        lf = lf_ref[rows, :]
        if t_real is not None:
            real = (lax.broadcasted_iota(jnp.int32, (L, D), 0) + c * L) < t_real
            lf = jnp.where((lax.broadcasted_iota(jnp.int32, lf.shape, 0) + c * L) < t_real, lf, 0.0)
        b = _time_cumsum(lf, L)
        for h in range(HG_HEADS):
            hs = slice(h * D, (h + 1) * D)
            bh = b[:, hs]
            qh = q_ref[rows, hs]
            kh = k_ref[rows, hs]
            if t_real is not None:
                kh = jnp.where(real, kh, 0.0)
            vb = v_ref[rows, hs].astype(BF16)
            blast = bh[L - 1:L, :]
            st = st_scr[h]
            qe = (qh * jnp.exp(bh)).astype(BF16)
            o = lax.dot_general(qe, st.astype(BF16), (((1,), (1,)), ((), ())), preferred_element_type=F32)
            scores = _intra_scores(bh, qh, kh, L)
            o = o + _dot(scores.astype(BF16), vb)
            kd = (kh * jnp.exp(blast - bh)).astype(BF16)
            upd = lax.dot_general(vb, kd, (((0,), (0,)), ((), ())), preferred_element_type=F32)
            st_scr[h] = st * jnp.exp(blast) + upd
            on = o * lax.rsqrt(jnp.mean(o * o, axis=-1, keepdims=True) + EPS) * hgn_ref[:, hs]
            mixf[rows, hs] = on * gt_ref[rows, hs]

    def seq_step(n):
        base = pl.multiple_of(n * tt, SUBLANES)
        if n_tiles == 1:
            load_state(n)
        else:
            pl.when(ti == 0)(lambda: load_state(n))
        n_chunks = tt // L
        if n_chunks == 1:
            chunk_step(0, base)
        else:
            lax.fori_loop(0, n_chunks, lambda c, _: (chunk_step(c, base), 0)[1], 0)

        rows_t = pl.ds(base, tt)
        ubuf[16:16 + tt, :] = zp_ref[rows_t, :]
        for gi, w in enumerate(POOL_WINDOWS):
            ls = slice(gi * POOL_GC, (gi + 1) * POOL_GC)
            cur = ubuf[16:16 + tt, ls]
            acc = cur
            for j in range(1, w):
                acc = acc + ubuf[16 - j:16 - j + tt, ls]
            d = acc / cnt_ref[:, ls] - cur
            y = _dot(d.astype(BF16), pw_ref[gi]) * ps_ref[:, ls]
            mixf[rows_t, HG_WIDTH + gi * POOL_GC:HG_WIDTH + (gi + 1) * POOL_GC] = y

        tr = tt if t_real is None else t_real

        def write_state():
            pout_ref[n] = ubuf[tr + 1:tr + 1 + POOL_STATE, :]
            for h in range(HG_HEADS):
                sout_ref[n, h] = st_scr[h].T

        if n_tiles == 1:
            write_state()
        else:
            pl.when(ti == n_tiles - 1)(write_state)
            ubuf[0:16, :] = ubuf[tt:tt + 16, :]

    if nseq == 1:
        seq_step(0)
    else:
        lax.fori_loop(0, nseq, lambda n, _: (seq_step(n), 0)[1], 0)
    mix_ref[...] = mixf[...].astype(BF16)


def _mix_call(proj, s0, pp, cnt, hgn, pw, ps, mix_prev, *, row_offset, n_seq_total, seq_len, tt, chunk, nseq,
              t_real, shared_state):
    m = proj[0].shape[0]
    n_tiles = seq_len // tt
    assert seq_len % tt == 0 and tt % chunk == 0 and (nseq == 1 or n_tiles == 1)
    assert n_seq_total % nseq == 0
    r = nseq * tt
    assert row_offset % r == 0
    blk0 = row_offset // r
    rows = lambda w: pl.BlockSpec((r, w), lambda bi, ti: (blk0 + bi * n_tiles + ti, 0))
    if shared_state:
        s_spec = pl.BlockSpec((1, HG_HEADS, HG_D, HG_D), lambda bi, ti: (0, 0, 0, 0))
        p_spec = pl.BlockSpec((1, POOL_STATE, POOL_WIDTH), lambda bi, ti: (0, 0, 0))
    else:
        s_spec = pl.BlockSpec((nseq, HG_HEADS, HG_D, HG_D), lambda bi, ti: (bi, 0, 0, 0))
        p_spec = pl.BlockSpec((nseq, POOL_STATE, POOL_WIDTH), lambda bi, ti: (bi, 0, 0))
    cnt_spec = pl.BlockSpec(cnt.shape, lambda bi, ti: (0, 0))
    kern = functools.partial(_mix_kernel, tt=tt, chunk=chunk, nseq=nseq, n_tiles=n_tiles, t_real=t_real,
                             shared_state=shared_state)
    in_specs = [rows(HG_WIDTH)] * 6 + [s_spec, p_spec, cnt_spec,
                                       pl.BlockSpec((1, HG_WIDTH), lambda bi, ti: (0, 0)),
                                       pl.BlockSpec((len(POOL_WINDOWS), POOL_GC, POOL_GC), lambda bi, ti: (0, 0, 0)),
                                       pl.BlockSpec((1, POOL_WIDTH), lambda bi, ti: (0, 0))]
    args = [*proj, s0, pp, cnt, hgn, pw, ps]
    aliases = {}
    if mix_prev is not None:
        in_specs.append(pl.BlockSpec(memory_space=pl.ANY))
        aliases = {len(args): 0}
        args.append(mix_prev)
    return pl.pallas_call(
        kern,
        grid=(n_seq_total // nseq, n_tiles),
        in_specs=in_specs,
        input_output_aliases=aliases,
        out_specs=[pl.BlockSpec((r, D_MODEL), lambda bi, ti: (blk0 + bi * n_tiles + ti, 0)),
                   pl.BlockSpec((nseq, HG_HEADS, HG_D, HG_D), lambda bi, ti: (bi, 0, 0, 0)),
                   pl.BlockSpec((nseq, POOL_STATE, POOL_WIDTH), lambda bi, ti: (bi, 0, 0))],
        out_shape=[jax.ShapeDtypeStruct((m, D_MODEL), BF16),
                   jax.ShapeDtypeStruct((n_seq_total, HG_HEADS, HG_D, HG_D), F32),
                   jax.ShapeDtypeStruct((n_seq_total, POOL_STATE, POOL_WIDTH), F32)],
        scratch_shapes=[pltpu.VMEM((HG_HEADS, HG_D, HG_D), F32),
                        pltpu.VMEM((16 + tt, POOL_WIDTH), F32),
                        pltpu.VMEM((r, D_MODEL), F32)],
        compiler_params=pltpu.CompilerParams(dimension_semantics=("arbitrary", "arbitrary"),
                                             vmem_limit_bytes=VMEM_LIMIT),
        name="mix",
    )(*args)


def _lower_bound_params(lb_logits):
    p = jax.nn.softmax(lb_logits.astype(F32), axis=0)
    cs = jnp.cumsum(p, axis=0)
    lb = cs - cs[0:1]
    rows = jnp.stack([1.0 - lb, jnp.log(lb), jnp.log1p(-lb)], axis=1)
    return jnp.pad(rows, ((0, 0), (0, SUBLANES - 3), (0, 0)))


def kernel(x_prompt, x_sample, state_hgrn, state_pool, meta, lb_logits, norm_ffn1, w_ffn1_in, w_ffn1_out,
           norm_mix, w_in, hg_norm, pool_w, pool_scale, w_out, norm_ffn2, w_ffn2_in, w_ffn2_out, norm_final):
    bp, sp, _ = x_prompt.shape
    bs, ss, _ = x_sample.shape
    ss_pad = SUBLANES
    n_prompt = bp * sp
    n_sample = bs * ss_pad
    meta_pad = ROW_TILE

    xs = jnp.pad(x_sample, ((0, 0), (0, ss_pad - ss), (0, 0))).reshape(n_sample, D_MODEL)
    xm = jnp.pad(meta.astype(F32), ((0, meta_pad - N_META), (0, 0)))
    x = jnp.concatenate([x_prompt.reshape(n_prompt, D_MODEL), xs, xm], axis=0)

    lbp = _lower_bound_params(lb_logits)
    wmax = jnp.repeat(jnp.asarray(POOL_WINDOWS, F32), POOL_GC)[None, :]
    cnt_meta = jnp.minimum(wmax, jnp.arange(1, meta_pad + 1, dtype=F32)[:, None])
    zero_s = jnp.zeros((1, HG_HEADS, HG_D, HG_D), F32)
    zero_p = jnp.zeros((1, POOL_STATE, POOL_WIDTH), F32)
    row1 = lambda a: a.reshape(1, -1)

    hg_p, pool_p, hg_s, pool_s = [], [], [], []
    for l in range(DEPTH):
        x1, *proj = _pre_call(x, row1(norm_ffn1[l]), w_ffn1_in[l].astype(BF16), w_ffn1_out[l].astype(BF16),
                              row1(norm_mix[l]), w_in[l].astype(BF16), lbp[l])
        common = (row1(hg_norm[l]), pool_w[l].astype(BF16), row1(pool_scale[l]))
        mix, s_m, p_m = _mix_call(proj, zero_s, zero_p, cnt_meta, *common, None,
                                  row_offset=n_prompt + n_sample, n_seq_total=1, seq_len=meta_pad,
                                  tt=meta_pad, chunk=64, nseq=1, t_real=N_META, shared_state=False)
        mix, s_p, p_p = _mix_call(proj, s_m, p_m, wmax, *common, mix,
                                  row_offset=0, n_seq_total=bp, seq_len=sp,
                                  tt=256, chunk=64, nseq=1, t_real=None, shared_state=True)
        mix, s_s, p_s = _mix_call(proj, state_hgrn[l], state_pool[l], wmax, *common, mix,
                                  row_offset=n_prompt, n_seq_total=bs, seq_len=ss_pad,
                                  tt=ss_pad, chunk=ss_pad, nseq=32, t_real=ss, shared_state=False)
        x = _post_call(x1, mix, w_out[l].astype(BF16), row1(norm_ffn2[l]), w_ffn2_in[l].astype(BF16),
                       w_ffn2_out[l].astype(BF16), row1(norm_final), final=(l == DEPTH - 1))
        hg_p.append(s_p)
        pool_p.append(p_p)
        hg_s.append(s_s)
        pool_s.append(p_s)

    y_prompt = x[:n_prompt].reshape(bp, sp, D_MODEL)
    y_sample = x[n_prompt:n_prompt + n_sample].reshape(bs, ss_pad, D_MODEL)[:, :ss]
    return (y_prompt, y_sample, jnp.stack(hg_p), jnp.stack(pool_p), jnp.stack(hg_s), jnp.stack(pool_s))
```

```python
import functools

import jax
import jax.numpy as jnp
from jax import lax
from jax.experimental import pallas as pl
from jax.experimental.pallas import tpu as pltpu

F32 = jnp.float32
BF16 = jnp.bfloat16

D_MODEL = 1024
DEPTH = 4
N_META = 16
HG_WIDTH = 512
HG_HEADS = 4
HG_D = 128
POOL_WIDTH = 512
POOL_WINDOWS = (2, 4, 8, 16)
POOL_GC = 128
POOL_STATE = 15
IN_COLS = 4 * HG_WIDTH + POOL_WIDTH
D_FF = 2816
EPS = 1e-6

SUBLANES = 8
MXU_DIM = 256
FF_CHUNK = MXU_DIM
ROW_TILE = 512
MIX_TILE = 256
MIX_CHUNK = 64
SAMPLE_SEQS = 32
SAMPLE_UNROLL = 1
VMEM_LIMIT = 56 * 1024 * 1024
NEG_BIG = -1e30


def _const_spec(shape):
    nd = len(shape)
    return pl.BlockSpec(shape, lambda *_: (0,) * nd, pipeline_mode=pl.Buffered(1))


def _rmsnorm(x, gain):
    return x * lax.rsqrt(jnp.mean(x * x, axis=-1, keepdims=True) + EPS) * gain


def _dot(a, b):
    return jnp.dot(a, b, preferred_element_type=F32)


def _silu(z):
    return z * (1.0 / (1.0 + jnp.exp(-z)))


def _ffn_half_step(x, gain, win_ref, wout_ref, acc_ref):
    hn = _rmsnorm(x, gain).astype(BF16)
    for c in range(D_FF // FF_CHUNK):
        lo = c * FF_CHUNK
        gate = _dot(hn, win_ref[:, lo:lo + FF_CHUNK])
        up = _dot(hn, win_ref[:, D_FF + lo:D_FF + lo + FF_CHUNK])
        part = _dot((_silu(gate) * up).astype(BF16), wout_ref[lo:lo + FF_CHUNK, :])
        if c == 0:
            acc_ref[...] = part
        else:
            acc_ref[...] += part
    return x + 0.5 * acc_ref[...]


def _group_tile_maps(n_p, n_s):
    return (lambda i: (jnp.minimum(i, n_p - 1), 0)), (lambda i: (jnp.clip(i - n_p, 0, n_s - 1), 0))


def _pre_kernel(*refs, split):
    n_x = 3 if split else 1
    g1_ref, w1i_ref, w1o_ref, gm_ref, win_ref, lbp_ref = refs[n_x:n_x + 6]
    x1_ref, q_ref, k_ref, lf_ref, v_ref, gt_ref, zp_ref, acc_ref = refs[n_x + 6:]
    if split:
        n_p, n_s = split
        i = pl.program_id(0)
        x = jnp.where(i < n_p, refs[0][...], jnp.where(i < n_p + n_s, refs[1][...], refs[2][...]))
    else:
        x = refs[0][...]
    x1 = _ffn_half_step(x, g1_ref[...], w1i_ref, w1o_ref, acc_ref)
    x1_ref[...] = x1
    hn = _rmsnorm(x1, gm_ref[...]).astype(BF16)
    W = HG_WIDTH
    q_ref[...] = _silu(_dot(hn, win_ref[:, 0:W]))
    zf = _dot(hn, win_ref[:, W:2 * W])
    one_m_lb = lbp_ref[0:1, :]
    log_lb = lbp_ref[1:2, :]
    log1m_lb = lbp_ref[2:3, :]
    e = jnp.exp(-jnp.abs(zf))
    k_ref[...] = one_m_lb * (jnp.where(zf >= 0.0, e, 1.0) * (1.0 / (1.0 + e)))
    c = log1m_lb + (jnp.minimum(zf, 0.0) - jnp.log1p(e))
    lf_ref[...] = jnp.maximum(log_lb, c) + jnp.log1p(jnp.exp(-jnp.abs(log_lb - c)))
    v_ref[...] = _dot(hn, win_ref[:, 2 * W:3 * W])
    gt_ref[...] = _silu(_dot(hn, win_ref[:, 3 * W:4 * W]))
    zp_ref[...] = _dot(hn, win_ref[:, 4 * W:5 * W])


def _pre_call(xs, n_rows, g1, w1i, w1o, gm, win, lbp):
    tm = ROW_TILE
    row = lambda w: pl.BlockSpec((tm, w), lambda i: (i, 0))
    split = None
    x_specs = [row(D_MODEL)]
    if len(xs) == 3:
        split = (xs[0].shape[0] // tm, xs[1].shape[0] // tm)
        pmap, smap = _group_tile_maps(*split)
        x_specs = [pl.BlockSpec((tm, D_MODEL), pmap), pl.BlockSpec((tm, D_MODEL), smap),
                   pl.BlockSpec((tm, D_MODEL), lambda i: (0, 0))]
    outs = [jax.ShapeDtypeStruct((n_rows, D_MODEL), F32)] + [jax.ShapeDtypeStruct((n_rows, HG_WIDTH), F32)] * 6
    return pl.pallas_call(
        functools.partial(_pre_kernel, split=split),
        grid=(n_rows // tm,),
        in_specs=x_specs + [_const_spec((1, D_MODEL)), _const_spec((D_MODEL, 2 * D_FF)),
                            _const_spec((D_FF, D_MODEL)), _const_spec((1, D_MODEL)),
                            _const_spec((D_MODEL, IN_COLS)), _const_spec((SUBLANES, HG_WIDTH))],
        out_specs=[row(D_MODEL)] + [row(HG_WIDTH)] * 6,
        out_shape=outs,
        scratch_shapes=[pltpu.VMEM((tm, D_MODEL), F32)],
        compiler_params=pltpu.CompilerParams(dimension_semantics=("arbitrary",),
                                             vmem_limit_bytes=VMEM_LIMIT),
        name="pre",
    )(*xs, g1, w1i, w1o, gm, win, lbp)


def _post_kernel(x1_ref, mix_ref, wo_ref, g2_ref, w2i_ref, w2o_ref, gf_ref, *rest, split):
    acc_ref = rest[-1]
    x2 = x1_ref[...] + _dot(mix_ref[...], wo_ref[...])
    x3 = _ffn_half_step(x2, g2_ref[...], w2i_ref, w2o_ref, acc_ref)
    if split is None:
        rest[0][...] = x3
        return
    n_p, n_s = split
    yp_ref, ys_ref = rest[0], rest[1]
    i = pl.program_id(0)
    y = _rmsnorm(x3, gf_ref[...])

    @pl.when(i < n_p)
    def _():
        yp_ref[...] = y

    @pl.when(jnp.logical_and(i >= n_p, i < n_p + n_s))
    def _():
        ys_ref[...] = y


def _post_call(x1_rows, mix_rows, wo, g2, w2i, w2o, gf, split_rows):
    m = x1_rows.shape[0]
    tm = ROW_TILE
    row = lambda w: pl.BlockSpec((tm, w), lambda i: (i, 0))
    if split_rows is None:
        split = None
        out_specs = row(D_MODEL)
        out_shape = jax.ShapeDtypeStruct((m, D_MODEL), F32)
    else:
        split = (split_rows[0] // tm, split_rows[1] // tm)
        pmap, smap = _group_tile_maps(*split)
        out_specs = [pl.BlockSpec((tm, D_MODEL), pmap), pl.BlockSpec((tm, D_MODEL), smap)]
        out_shape = [jax.ShapeDtypeStruct((r, D_MODEL), F32) for r in split_rows]
    return pl.pallas_call(
        functools.partial(_post_kernel, split=split),
        grid=(m // tm,),
        in_specs=[row(D_MODEL), row(D_MODEL), _const_spec((D_MODEL, D_MODEL)), _const_spec((1, D_MODEL)),
                  _const_spec((D_MODEL, 2 * D_FF)), _const_spec((D_FF, D_MODEL)), _const_spec((1, D_MODEL))],
        out_specs=out_specs,
        out_shape=out_shape,
        scratch_shapes=[pltpu.VMEM((tm, D_MODEL), F32)],
        compiler_params=pltpu.CompilerParams(dimension_semantics=("arbitrary",),
                                             vmem_limit_bytes=VMEM_LIMIT),
        name="post",
    )(x1_rows, mix_rows, wo, g2, w2i, w2o, gf)


def _intra_scores(bh, qh, kh, L):
    D = HG_D
    row_l = lax.broadcasted_iota(jnp.int32, (L, D), 0)
    row_i = lax.broadcasted_iota(jnp.int32, (L, L), 0)
    col_i = lax.broadcasted_iota(jnp.int32, (L, L), 1)
    scores = jnp.zeros((L, L), F32)
    half = L // 2
    while half >= SUBLANES:
        pair = 2 * half
        pieces = [jnp.broadcast_to(bh[p * pair + half - 1:p * pair + half, :], (pair, D))
                  for p in range(L // pair)]
        bref = pieces[0] if len(pieces) == 1 else jnp.concatenate(pieces, axis=0)
        second = (row_l & half) != 0
        qt = (qh * jnp.exp(jnp.where(second, bh - bref, NEG_BIG))).astype(BF16)
        kt = (kh * jnp.exp(jnp.where(second, NEG_BIG, bref - bh))).astype(BF16)
        blk = lax.dot_general(qt, kt, (((1,), (1,)), ((), ())), preferred_element_type=F32)
        same_pair = (row_i & -pair) == (col_i & -pair)
        scores = scores + jnp.where(same_pair, blk, 0.0)
        half //= 2

    nb = L // SUBLANES
    b3 = bh.reshape(nb, SUBLANES, D)
    q3 = qh.reshape(nb, SUBLANES, D)
    k3 = kh.reshape(nb, SUBLANES, D)
    t_in = lax.broadcasted_iota(jnp.int32, (nb, SUBLANES, D), 1)
    lane = lax.broadcasted_iota(jnp.int32, (nb, SUBLANES, L), 2)
    blk0 = lax.broadcasted_iota(jnp.int32, (nb, SUBLANES, L), 0) * SUBLANES
    diag = jnp.zeros((nb, SUBLANES, L), F32)
    for s in range(SUBLANES):
        bs = b3[:, s:s + 1, :]
        ks = k3[:, s:s + 1, :]
        p = q3 * ks * jnp.exp(jnp.where(t_in >= s, b3 - bs, NEG_BIG))
        col = jnp.sum(p, axis=-1, keepdims=True)
        diag = jnp.where(lane == blk0 + s, col, diag)
    return scores + diag.reshape(L, L)


def _time_cumsum(lf, L):
    if L == SUBLANES:
        rows = [lf[0:1, :]]
        for r in range(1, L):
            rows.append(rows[-1] + lf[r:r + 1, :])
        return jnp.concatenate(rows, axis=0)
    tri = (lax.broadcasted_iota(jnp.int32, (L, L), 0) >= lax.broadcasted_iota(jnp.int32, (L, L), 1)).astype(BF16)
    l1 = lf.astype(BF16)
    r1 = lf - l1.astype(F32)
    l2 = r1.astype(BF16)
    l3 = (r1 - l2.astype(F32)).astype(BF16)
    return _dot(tri, l1) + _dot(tri, l2) + _dot(tri, l3)


def _mix_kernel(q_ref, k_ref, lf_ref, v_ref, gt_ref, zp_ref, s0_ref, pp_ref, cnt_ref, hgn_ref, pw_ref, ps_ref,
                *rest, tt, chunk, nseq, unroll, n_tiles, t_real, shared_state):
    mix_ref, sout_ref, pout_ref, st_scr, ubuf, mixf = rest[-6:]
    D = HG_D
    L = chunk
    ti = pl.program_id(1)
    tr = tt if t_real is None else t_real
    n_chunks = pl.cdiv(tr, L)

    def chunk_step(c, base, slot):
        r0 = pl.multiple_of(base + c * L, SUBLANES)
        rows = pl.ds(r0, L)
        lf = lf_ref[rows, :]
        if t_real is not None:
            real = (lax.broadcasted_iota(jnp.int32, (L, D), 0) + c * L) < t_real
            lf = jnp.where((lax.broadcasted_iota(jnp.int32, lf.shape, 0) + c * L) < t_real, lf, 0.0)
        b = _time_cumsum(lf, L)
        for h in range(HG_HEADS):
            hs = slice(h * D, (h + 1) * D)
            bh = b[:, hs]
            qh = q_ref[rows, hs]
            kh = k_ref[rows, hs]
            if t_real is not None:
                kh = jnp.where(real, kh, 0.0)
            vb = v_ref[rows, hs].astype(BF16)
            blast = bh[L - 1:L, :]
            st = st_scr[slot, h]
            qe = (qh * jnp.exp(bh)).astype(BF16)
            o = lax.dot_general(qe, st.astype(BF16), (((1,), (1,)), ((), ())), preferred_element_type=F32)
            scores = _intra_scores(bh, qh, kh, L)
            o = o + _dot(scores.astype(BF16), vb)
            kd = (kh * jnp.exp(blast - bh)).astype(BF16)
            upd = lax.dot_general(vb, kd, (((0,), (0,)), ((), ())), preferred_element_type=F32)
            st_scr[slot, h] = st * jnp.exp(blast) + upd
            on = o * lax.rsqrt(jnp.mean(o * o, axis=-1, keepdims=True) + EPS) * hgn_ref[:, hs]
            mixf[rows, hs] = on * gt_ref[rows, hs]

    def seq_step(n, slot):
        base = pl.multiple_of(n * tt, SUBLANES)
        sidx = 0 if shared_state else n

        def load_state():
            for h in range(HG_HEADS):
                st_scr[slot, h] = s0_ref[sidx, h].T
            ubuf[slot, 1:1 + POOL_STATE, :] = pp_ref[sidx]

        if n_tiles == 1:
            load_state()
        else:
            pl.when(ti == 0)(load_state)
        if n_chunks == 1:
            chunk_step(0, base, slot)
        else:
            lax.fori_loop(0, n_chunks, lambda c, _: (chunk_step(c, base, slot), 0)[1], 0)
        if n_chunks * L < tt:
            mixf[pl.ds(base + n_chunks * L, tt - n_chunks * L), 0:HG_WIDTH] = jnp.zeros(
                (tt - n_chunks * L, HG_WIDTH), F32)

        rows_t = pl.ds(base, tt)
        ubuf[slot, 16:16 + tt, :] = zp_ref[rows_t, :]
        for gi, w in enumerate(POOL_WINDOWS):
            ls = slice(gi * POOL_GC, (gi + 1) * POOL_GC)
            cur = ubuf[slot, 16:16 + tt, ls]
            acc = cur
            for j in range(1, w):
                acc = acc + ubuf[slot, 16 - j:16 - j + tt, ls]
            d = acc / cnt_ref[:, ls] - cur
            y = _dot(d.astype(BF16), pw_ref[gi]) * ps_ref[:, ls]
            mixf[rows_t, HG_WIDTH + gi * POOL_GC:HG_WIDTH + (gi + 1) * POOL_GC] = y

        def write_state():
            pout_ref[n] = ubuf[slot, tr + 1:tr + 1 + POOL_STATE, :]
            for h in range(HG_HEADS):
                sout_ref[n, h] = st_scr[slot, h].T

        if n_tiles == 1:
            write_state()
        else:
            pl.when(ti == n_tiles - 1)(write_state)
            ubuf[slot, 0:16, :] = ubuf[slot, tt:tt + 16, :]

    def group_step(g):
        for j in range(unroll):
            seq_step(g * unroll + j, j)

    if nseq == unroll:
        group_step(0)
    else:
        lax.fori_loop(0, nseq // unroll, lambda g, _: (group_step(g), 0)[1], 0)
    mix_ref[...] = mixf[...].astype(BF16)


def _mix_call(proj, s0, s0_layer, pp, pp_layer, cnt, hgn, pw, ps, prev, *, layer, row_offset, n_seq_total,
              seq_len, tt, chunk, nseq, unroll, t_real, shared_state):
    m = proj[0].shape[0]
    n_tiles = seq_len // tt
    assert seq_len % tt == 0 and (nseq == unroll == 1 or n_tiles == 1) and nseq % unroll == 0
    assert n_seq_total % nseq == 0
    r = nseq * tt
    assert row_offset % r == 0
    blk0 = row_offset // r
    rows = lambda w: pl.BlockSpec((r, w), lambda bi, ti: (blk0 + bi * n_tiles + ti, 0))
    sdims = (HG_HEADS, HG_D, HG_D)
    pdims = (POOL_STATE, POOL_WIDTH)
    if shared_state:
        s_spec = pl.BlockSpec((None, 1) + sdims, lambda bi, ti: (s0_layer, 0, 0, 0, 0))
        p_spec = pl.BlockSpec((None, 1) + pdims, lambda bi, ti: (pp_layer, 0, 0, 0))
    else:
        s_spec = pl.BlockSpec((None, nseq) + sdims, lambda bi, ti: (s0_layer, bi, 0, 0, 0))
        p_spec = pl.BlockSpec((None, nseq) + pdims, lambda bi, ti: (pp_layer, bi, 0, 0))
    in_specs = [rows(HG_WIDTH)] * 6 + [s_spec, p_spec,
                                       pl.BlockSpec(cnt.shape, lambda bi, ti: (0, 0)),
                                       pl.BlockSpec((1, HG_WIDTH), lambda bi, ti: (0, 0)),
                                       pl.BlockSpec((len(POOL_WINDOWS), POOL_GC, POOL_GC), lambda bi, ti: (0, 0, 0)),
                                       pl.BlockSpec((1, POOL_WIDTH), lambda bi, ti: (0, 0))]
    args = [*proj, s0, pp, cnt, hgn, pw, ps]
    aliases = {}
    for out_idx, buf in enumerate(prev or ()):
        if buf is not None:
            in_specs.append(pl.BlockSpec(memory_space=pl.ANY))
            aliases[len(args)] = out_idx
            args.append(buf)
    kern = functools.partial(_mix_kernel, tt=tt, chunk=chunk, nseq=nseq, unroll=unroll, n_tiles=n_tiles,
                             t_real=t_real, shared_state=shared_state)
    return pl.pallas_call(
        kern,
        grid=(n_seq_total // nseq, n_tiles),
        in_specs=in_specs,
        input_output_aliases=aliases,
        out_specs=[pl.BlockSpec((r, D_MODEL), lambda bi, ti: (blk0 + bi * n_tiles + ti, 0)),
                   pl.BlockSpec((None, nseq) + sdims, lambda bi, ti: (layer, bi, 0, 0, 0)),
                   pl.BlockSpec((None, nseq) + pdims, lambda bi, ti: (layer, bi, 0, 0))],
        out_shape=[jax.ShapeDtypeStruct((m, D_MODEL), BF16),
                   jax.ShapeDtypeStruct((DEPTH, n_seq_total) + sdims, F32),
                   jax.ShapeDtypeStruct((DEPTH, n_seq_total) + pdims, F32)],
        scratch_shapes=[pltpu.VMEM((unroll,) + sdims, F32),
                        pltpu.VMEM((unroll, 16 + tt, POOL_WIDTH), F32),
                        pltpu.VMEM((r, D_MODEL), F32)],
        compiler_params=pltpu.CompilerParams(dimension_semantics=("arbitrary", "arbitrary"),
                                             vmem_limit_bytes=VMEM_LIMIT),
        name="mix",
    )(*args)


def _lower_bound_params(lb_logits):
    p = jax.nn.softmax(lb_logits.astype(F32), axis=0)
    cs = jnp.cumsum(p, axis=0)
    lb = cs - cs[0:1]
    rows = jnp.stack([1.0 - lb, jnp.log(lb), jnp.log1p(-lb)], axis=1)
    return jnp.pad(rows, ((0, 0), (0, SUBLANES - 3), (0, 0)))


def kernel(x_prompt, x_sample, state_hgrn, state_pool, meta, lb_logits, norm_ffn1, w_ffn1_in, w_ffn1_out,
           norm_mix, w_in, hg_norm, pool_w, pool_scale, w_out, norm_ffn2, w_ffn2_in, w_ffn2_out, norm_final):
    bp, sp, _ = x_prompt.shape
    bs, ss, _ = x_sample.shape
    ss_pad = SUBLANES
    n_prompt = bp * sp
    n_sample = bs * ss_pad
    n_meta = ROW_TILE
    n_rows = n_prompt + n_sample + n_meta

    xp = x_prompt.reshape(n_prompt, D_MODEL)
    xs = jnp.pad(x_sample, ((0, 0), (0, ss_pad - ss), (0, 0))).reshape(n_sample, D_MODEL)
    xm = jnp.pad(meta.astype(F32), ((0, n_meta - N_META), (0, 0)))

    lbp = _lower_bound_params(lb_logits)
    wmax = jnp.repeat(jnp.asarray(POOL_WINDOWS, F32), POOL_GC)[None, :]
    cnt_meta = jnp.minimum(wmax, jnp.arange(1, n_meta + 1, dtype=F32)[:, None])
    zero_s = jnp.zeros((1, 1, HG_HEADS, HG_D, HG_D), F32)
    zero_p = jnp.zeros((1, 1, POOL_STATE, POOL_WIDTH), F32)
    row1 = lambda a: a.reshape(1, -1)

    x_in = [xp, xs, xm]
    st_m = st_p = st_s = (None, None)
    for l in range(DEPTH):
        x1, *proj = _pre_call(x_in, n_rows, row1(norm_ffn1[l]), w_ffn1_in[l].astype(BF16),
                              w_ffn1_out[l].astype(BF16), row1(norm_mix[l]), w_in[l].astype(BF16), lbp[l])
        common = (row1(hg_norm[l]), pool_w[l].astype(BF16), row1(pool_scale[l]))
        mix, *st_m = _mix_call(proj, zero_s, 0, zero_p, 0, cnt_meta, *common, (None, *st_m),
                               layer=l, row_offset=n_prompt + n_sample, n_seq_total=1, seq_len=n_meta,
                               tt=n_meta, chunk=MIX_CHUNK, nseq=1, unroll=1, t_real=N_META, shared_state=False)
        mix, *st_p = _mix_call(proj, st_m[0], l, st_m[1], l, wmax, *common, (mix, *st_p),
                               layer=l, row_offset=0, n_seq_total=bp, seq_len=sp,
                               tt=MIX_TILE, chunk=MIX_CHUNK, nseq=1, unroll=1, t_real=None, shared_state=True)
        mix, *st_s = _mix_call(proj, state_hgrn, l, state_pool, l, wmax, *common, (mix, *st_s),
                               layer=l, row_offset=n_prompt, n_seq_total=bs, seq_len=ss_pad,
                               tt=ss_pad, chunk=ss_pad, nseq=SAMPLE_SEQS, unroll=SAMPLE_UNROLL, t_real=ss,
                               shared_state=False)
        last = l == DEPTH - 1
        x_out = _post_call(x1, mix, w_out[l].astype(BF16), row1(norm_ffn2[l]), w_ffn2_in[l].astype(BF16),
                           w_ffn2_out[l].astype(BF16), row1(norm_final), (n_prompt, n_sample) if last else None)
        x_in = [x_out]

    y_prompt = x_out[0].reshape(bp, sp, D_MODEL)
    y_sample = x_out[1].reshape(bs, ss_pad, D_MODEL)[:, :ss]
    return (y_prompt, y_sample, st_p[0], st_p[1], st_s[0], st_s[1])
```

```python
import functools

import jax
import jax.numpy as jnp
from jax import lax
from jax.experimental import pallas as pl
from jax.experimental.pallas import tpu as pltpu

F32 = jnp.float32
BF16 = jnp.bfloat16

D_MODEL = 1024
DEPTH = 4
N_META = 16
HG_WIDTH = 512
HG_HEADS = 4
HG_D = 128
POOL_WIDTH = 512
POOL_WINDOWS = (2, 4, 8, 16)
POOL_GC = 128
POOL_STATE = 15
IN_COLS = 4 * HG_WIDTH + POOL_WIDTH
D_FF = 2816
EPS = 1e-6
N_PROJ = 6

SUBLANES = 8
MXU_DIM = 256
FF_CHUNK = MXU_DIM
ROW_TILE = 512
MIX_CHUNK = 64
SAMPLE_SEQS = 32
VMEM_LIMIT = 56 * 1024 * 1024
NEG_BIG = -1e30

SDIMS = (HG_HEADS, HG_D, HG_D)
PDIMS = (POOL_STATE, POOL_WIDTH)


def _layer_spec(shape, layer):
    nd = len(shape)
    return pl.BlockSpec((None,) + tuple(shape), lambda *_: (layer,) + (0,) * nd, pipeline_mode=pl.Buffered(1))


def _fixed_spec(shape):
    nd = len(shape)
    return pl.BlockSpec(tuple(shape), lambda *_: (0,) * nd)


def _rmsnorm(x, gain):
    return x * lax.rsqrt(jnp.mean(x * x, axis=-1, keepdims=True) + EPS) * gain


def _dot(a, b):
    return jnp.dot(a, b, preferred_element_type=F32)


def _silu(z):
    return z * (1.0 / (1.0 + jnp.exp(-z)))


def _run(steps):
    for _ in steps:
        pass


def _interleave(steps_a, n_a, steps_b, n_b):
    done_b = 0
    for i, _ in enumerate(steps_a):
        while done_b < ((i + 1) * n_b) // n_a:
            next(steps_b)
            done_b += 1
    _run(steps_b)


def _ffn_half_steps(x, gain, win_ref, wout_ref, acc_ref):
    hn = _rmsnorm(x, gain).astype(BF16)
    for c in range(D_FF // FF_CHUNK):
        lo = c * FF_CHUNK
        gate = _dot(hn, win_ref[:, lo:lo + FF_CHUNK])
        up = _dot(hn, win_ref[:, D_FF + lo:D_FF + lo + FF_CHUNK])
        yield
        part = _dot((_silu(gate) * up).astype(BF16), wout_ref[lo:lo + FF_CHUNK, :])
        if c == 0:
            acc_ref[...] = part
        else:
            acc_ref[...] += part
        yield


N_FFN_STEPS = 2 * (D_FF // FF_CHUNK)
N_PRE_STEPS = N_FFN_STEPS + 5


def _pre_rows_steps(x, g1_ref, w1i_ref, w1o_ref, gm_ref, win_ref, lbp_ref, acc_ref, x1_ref, proj_refs):
    q_ref, k_ref, lf_ref, v_ref, gt_ref, zp_ref = proj_refs
    yield from _ffn_half_steps(x, g1_ref[...], w1i_ref, w1o_ref, acc_ref)
    x1 = x + 0.5 * acc_ref[...]
    x1_ref[...] = x1
    hn = _rmsnorm(x1, gm_ref[...]).astype(BF16)
    W = HG_WIDTH
    q_ref[...] = _silu(_dot(hn, win_ref[:, 0:W]))
    yield
    zf = _dot(hn, win_ref[:, W:2 * W])
    one_m_lb = lbp_ref[0:1, :]
    log_lb = lbp_ref[1:2, :]
    log1m_lb = lbp_ref[2:3, :]
    e = jnp.exp(-jnp.abs(zf))
    k_ref[...] = one_m_lb * (jnp.where(zf >= 0.0, e, 1.0) * (1.0 / (1.0 + e)))
    c = log1m_lb + (jnp.minimum(zf, 0.0) - jnp.log1p(e))
    lf_ref[...] = jnp.maximum(log_lb, c) + jnp.log1p(jnp.exp(-jnp.abs(log_lb - c)))
    yield
    v_ref[...] = _dot(hn, win_ref[:, 2 * W:3 * W])
    yield
    gt_ref[...] = _silu(_dot(hn, win_ref[:, 3 * W:4 * W]))
    yield
    zp_ref[...] = _dot(hn, win_ref[:, 4 * W:5 * W])
    yield


def _pre_weight_specs(layer):
    return [_layer_spec((1, D_MODEL), layer), _layer_spec((D_MODEL, 2 * D_FF), layer),
            _layer_spec((D_FF, D_MODEL), layer), _layer_spec((1, D_MODEL), layer),
            _layer_spec((D_MODEL, IN_COLS), layer), _layer_spec((SUBLANES, HG_WIDTH), layer)]


def _intra_block_operands(bh, qh, kh, L):
    D = HG_D
    row_l = lax.broadcasted_iota(jnp.int32, (L, D), 0)
    operands = []
    half = L // 2
    while half >= SUBLANES:
        pair = 2 * half
        pieces = [jnp.broadcast_to(bh[p * pair + half - 1:p * pair + half, :], (pair, D))
                  for p in range(L // pair)]
        bref = pieces[0] if len(pieces) == 1 else jnp.concatenate(pieces, axis=0)
        second = (row_l & half) != 0
        qt = (qh * jnp.exp(jnp.where(second, bh - bref, NEG_BIG))).astype(BF16)
        kt = (kh * jnp.exp(jnp.where(second, NEG_BIG, bref - bh))).astype(BF16)
        operands.append((qt, kt))
        half //= 2
    return operands


def _dot_nt(a, b):
    return lax.dot_general(a, b, (((1,), (1,)), ((), ())), preferred_element_type=F32)


def _sum_block_products(products, L):
    row_i = lax.broadcasted_iota(jnp.int32, (L, L), 0)
    col_i = lax.broadcasted_iota(jnp.int32, (L, L), 1)
    scores = jnp.zeros((L, L), F32)
    pair = L
    for blk in products:
        scores = scores + jnp.where((row_i & -pair) == (col_i & -pair), blk, 0.0)
        pair //= 2
    return scores


def _intra_diag_scores(bh, qh, kh, L):
    D = HG_D
    nb = L // SUBLANES
    b3 = bh.reshape(nb, SUBLANES, D)
    q3 = qh.reshape(nb, SUBLANES, D)
    k3 = kh.reshape(nb, SUBLANES, D)
    t_in = lax.broadcasted_iota(jnp.int32, (nb, SUBLANES, D), 1)
    lane = lax.broadcasted_iota(jnp.int32, (nb, SUBLANES, L), 2)
    blk0 = lax.broadcasted_iota(jnp.int32, (nb, SUBLANES, L), 0) * SUBLANES
    diag = jnp.zeros((nb, SUBLANES, L), F32)
    for s in range(SUBLANES):
        bs = b3[:, s:s + 1, :]
        ks = k3[:, s:s + 1, :]
        p = q3 * ks * jnp.exp(jnp.where(t_in >= s, b3 - bs, NEG_BIG))
        col = jnp.sum(p, axis=-1, keepdims=True)
        diag = jnp.where(lane == blk0 + s, col, diag)
    return diag.reshape(L, L)


def _time_cumsum(lf, L):
    if L == SUBLANES:
        rows = [lf[0:1, :]]
        for r in range(1, L):
            rows.append(rows[-1] + lf[r:r + 1, :])
        return jnp.concatenate(rows, axis=0)
    tri = (lax.broadcasted_iota(jnp.int32, (L, L), 0) >= lax.broadcasted_iota(jnp.int32, (L, L), 1)).astype(BF16)
    l1 = lf.astype(BF16)
    r1 = lf - l1.astype(F32)
    l2 = r1.astype(BF16)
    l3 = (r1 - l2.astype(F32)).astype(BF16)
    return _dot(tri, l1) + _dot(tri, l2) + _dot(tri, l3)


def _mix_sequence_tile_steps(proj, consts, mixf, st_scr, ubuf, *, base, tt, chunk, t_real, unrolled):
    q_ref, k_ref, lf_ref, v_ref, gt_ref, zp_ref = proj
    cnt_ref, hgn_ref, pw_ref, ps_ref = consts
    D = HG_D
    L = chunk
    tr = tt if t_real is None else t_real
    n_chunks = pl.cdiv(tr, L)

    rows_t = pl.ds(base, tt)
    ubuf[16:16 + tt, :] = zp_ref[rows_t, :]
    for gi, w in enumerate(POOL_WINDOWS):
        ls = slice(gi * POOL_GC, (gi + 1) * POOL_GC)
        cur = ubuf[16:16 + tt, ls]
        acc = cur
        for j in range(1, w):
            acc = acc + ubuf[16 - j:16 - j + tt, ls]
        d = acc / cnt_ref[:, ls] - cur
        y = _dot(d.astype(BF16), pw_ref[gi]) * ps_ref[:, ls]
        mixf[rows_t, HG_WIDTH + gi * POOL_GC:HG_WIDTH + (gi + 1) * POOL_GC] = y
        yield

    def chunk_rows(c):
        r0 = base + c * L
        return pl.ds(r0 if isinstance(r0, int) else pl.multiple_of(r0, SUBLANES), L)

    def chunk_cumsum(c):
        lf = lf_ref[chunk_rows(c), :]
        if t_real is not None:
            lf = jnp.where((lax.broadcasted_iota(jnp.int32, lf.shape, 0) + c * L) < t_real, lf, 0.0)
        return _time_cumsum(lf, L)

    def chunk_steps(c, b, next_b):
        rows = chunk_rows(c)
        prepared = []
        for h in range(HG_HEADS):
            hs = slice(h * D, (h + 1) * D)
            bh = b[:, hs]
            qh = q_ref[rows, hs]
            kh = k_ref[rows, hs]
            if t_real is not None:
                kh = jnp.where((lax.broadcasted_iota(jnp.int32, (L, D), 0) + c * L) < t_real, kh, 0.0)
            prepared.append((hs, bh, qh, kh, _intra_block_operands(bh, qh, kh, L), (qh * jnp.exp(bh)).astype(BF16)))
        yield
        heads = []
        for h, (hs, bh, qh, kh, operands, qe) in enumerate(prepared):
            products = [_dot_nt(qt, kt) for qt, kt in operands]
            st = st_scr[h]
            o_inter = _dot_nt(qe, st.astype(BF16))
            heads.append((hs, bh, kh, st, o_inter, products, _intra_diag_scores(bh, qh, kh, L)))
        if next_b is not None:
            next_b.append(chunk_cumsum(c + 1))
        yield
        results = []
        for hs, bh, kh, st, o_inter, products, diag in heads:
            vb = v_ref[rows, hs].astype(BF16)
            blast = bh[L - 1:L, :]
            scores = _sum_block_products(products, L) + diag
            o = o_inter + _dot(scores.astype(BF16), vb)
            kd = (kh * jnp.exp(blast - bh)).astype(BF16)
            upd = lax.dot_general(vb, kd, (((0,), (0,)), ((), ())), preferred_element_type=F32)
            results.append((hs, st, blast, o, upd))
        yield
        for h, (hs, st, blast, o, upd) in enumerate(results):
            st_scr[h] = st * jnp.exp(blast) + upd
            on = o * lax.rsqrt(jnp.mean(o * o, axis=-1, keepdims=True) + EPS) * hgn_ref[:, hs]
            mixf[rows, hs] = on * gt_ref[rows, hs]

    if unrolled or n_chunks == 1:
        b = chunk_cumsum(0)
        yield
        for c in range(n_chunks):
            next_b = [] if c + 1 < n_chunks else None
            yield from chunk_steps(c, b, next_b)
            b = next_b[0] if next_b else None
    else:
        lax.fori_loop(0, n_chunks, lambda c, _: (_run(chunk_steps(c, chunk_cumsum(c), None)), 0)[1], 0)
    if n_chunks * L < tt:
        mixf[pl.ds(base + n_chunks * L, tt - n_chunks * L), 0:HG_WIDTH] = jnp.zeros(
            (tt - n_chunks * L, HG_WIDTH), F32)


def _load_state(s0_ref, pp_ref, sidx, st_scr, ubuf):
    for h in range(HG_HEADS):
        st_scr[h] = s0_ref[sidx, h].T
    ubuf[1:1 + POOL_STATE, :] = pp_ref[sidx]


def _store_state(sout_ref, pout_ref, n, st_scr, ubuf, t_rows):
    pout_ref[n] = ubuf[t_rows + 1:t_rows + 1 + POOL_STATE, :]
    for h in range(HG_HEADS):
        sout_ref[n, h] = st_scr[h].T


def _mix_const_specs(cnt, layer):
    return [_fixed_spec(cnt.shape), _layer_spec((1, HG_WIDTH), layer),
            _layer_spec((len(POOL_WINDOWS), POOL_GC, POOL_GC), layer), _layer_spec((1, POOL_WIDTH), layer)]


def _pre_kernel(*refs, split):
    n_x = 2 if split else 1
    weights = refs[n_x:n_x + 6]
    x1_ref = refs[n_x + 6]
    proj_refs = refs[n_x + 7:n_x + 7 + N_PROJ]
    acc_ref = refs[-1]
    if split:
        x = jnp.where(pl.program_id(0) < split, refs[0][...], refs[1][...])
    else:
        x = refs[0][...]
    _run(_pre_rows_steps(x, *weights, acc_ref, x1_ref, proj_refs))


def _pre_call(xs, n_rows, weights, layer):
    tm = ROW_TILE
    row = lambda w: pl.BlockSpec((tm, w), lambda i: (i, 0))
    split = None
    x_specs = [row(D_MODEL)]
    if len(xs) == 2:
        split = xs[0].shape[0] // tm
        n2 = xs[1].shape[0] // tm
        x_specs = [pl.BlockSpec((tm, D_MODEL), lambda i: (jnp.minimum(i, split - 1), 0)),
                   pl.BlockSpec((tm, D_MODEL), lambda i: (jnp.clip(i - split, 0, n2 - 1), 0))]
    outs = [jax.ShapeDtypeStruct((n_rows, D_MODEL), F32)] + [jax.ShapeDtypeStruct((n_rows, HG_WIDTH), F32)] * N_PROJ
    return pl.pallas_call(
        functools.partial(_pre_kernel, split=split),
        grid=(n_rows // tm,),
        in_specs=x_specs + _pre_weight_specs(layer),
        out_specs=[row(D_MODEL)] + [row(HG_WIDTH)] * N_PROJ,
        out_shape=outs,
        scratch_shapes=[pltpu.VMEM((tm, D_MODEL), F32)],
        compiler_params=pltpu.CompilerParams(dimension_semantics=("arbitrary",),
                                             vmem_limit_bytes=VMEM_LIMIT),
        name="pre",
    )(*xs, *weights)


def _premix_kernel(x_ref, g1_ref, w1i_ref, w1o_ref, gm_ref, win_ref, lbp_ref,
                   s0_ref, pp_ref, cnt_ref, hgn_ref, pw_ref, ps_ref, *rest, tiles_per_seq):
    n_alias = len(rest) - 4 - (N_PROJ + 4)
    x1_ref, mix_ref, sout_ref, pout_ref = rest[n_alias:n_alias + 4]
    proj_scr = rest[n_alias + 4:n_alias + 4 + N_PROJ]
    acc_ref, st_scr, ubuf, mixf = rest[n_alias + 4 + N_PROJ:]
    tm = ROW_TILE
    s = pl.program_id(0)
    t = jnp.maximum(s - 1, 0)
    pos = lax.rem(t, tiles_per_seq)

    @pl.when(s == 0)
    def _():
        for r in proj_scr:
            r[...] = jnp.zeros(r.shape, F32)

    @pl.when(pos == 0)
    def _():
        _load_state(s0_ref, pp_ref, 0, st_scr, ubuf)

    slot_mix = lax.rem(s + 1, 2)
    slot_pre = lax.rem(s, 2)
    mix_steps = _mix_sequence_tile_steps([r.at[slot_mix] for r in proj_scr], (cnt_ref, hgn_ref, pw_ref, ps_ref),
                                         mixf, st_scr, ubuf, base=0, tt=tm, chunk=MIX_CHUNK, t_real=None,
                                         unrolled=True)
    pre_steps = _pre_rows_steps(x_ref[...], g1_ref, w1i_ref, w1o_ref, gm_ref, win_ref, lbp_ref, acc_ref, x1_ref,
                                [r.at[slot_pre] for r in proj_scr])
    n_mix_steps = len(POOL_WINDOWS) + 1 + 3 * (tm // MIX_CHUNK)
    _interleave(pre_steps, N_PRE_STEPS, mix_steps, n_mix_steps)
    mix_ref[...] = mixf[...].astype(BF16)

    @pl.when(jnp.logical_and(s > 0, pos == tiles_per_seq - 1))
    def _():
        _store_state(sout_ref, pout_ref, 0, st_scr, ubuf, tm)

    ubuf[0:16, :] = ubuf[tm:tm + 16, :]


def _premix_call(x_rows, weights, s0, pp, cnt, mix_consts, prev, *, layer, seq_len):
    m = x_rows.shape[0]
    tm = ROW_TILE
    n = m // tm
    tiles_per_seq = seq_len // tm
    n_seq = m // seq_len
    mix_tile = lambda s: jnp.maximum(s - 1, 0)
    in_specs = ([pl.BlockSpec((tm, D_MODEL), lambda s: (jnp.minimum(s, n - 1), 0))] + _pre_weight_specs(layer)
                + [pl.BlockSpec((None, 1) + SDIMS, lambda s: (layer, 0, 0, 0, 0)),
                   pl.BlockSpec((None, 1) + PDIMS, lambda s: (layer, 0, 0, 0))]
                + _mix_const_specs(cnt, layer))
    args = [x_rows, *weights, s0, pp, cnt, *mix_consts]
    aliases = {}
    for out_idx, buf in zip((2, 3), prev or ()):
        in_specs.append(pl.BlockSpec(memory_space=pl.ANY))
        aliases[len(args)] = out_idx
        args.append(buf)
    return pl.pallas_call(
        functools.partial(_premix_kernel, tiles_per_seq=tiles_per_seq),
        grid=(n + 1,),
        in_specs=in_specs,
        input_output_aliases=aliases,
        out_specs=[pl.BlockSpec((tm, D_MODEL), lambda s: (jnp.minimum(s, n - 1), 0)),
                   pl.BlockSpec((tm, D_MODEL), lambda s: (mix_tile(s), 0)),
                   pl.BlockSpec((None, 1) + SDIMS, lambda s: (layer, mix_tile(s) // tiles_per_seq, 0, 0, 0)),
                   pl.BlockSpec((None, 1) + PDIMS, lambda s: (layer, mix_tile(s) // tiles_per_seq, 0, 0))],
        out_shape=[jax.ShapeDtypeStruct((m, D_MODEL), F32), jax.ShapeDtypeStruct((m, D_MODEL), BF16),
                   jax.ShapeDtypeStruct((DEPTH, n_seq) + SDIMS, F32),
                   jax.ShapeDtypeStruct((DEPTH, n_seq) + PDIMS, F32)],
        scratch_shapes=[pltpu.VMEM((2, tm, HG_WIDTH), F32)] * N_PROJ
        + [pltpu.VMEM((tm, D_MODEL), F32), pltpu.VMEM(SDIMS, F32),
           pltpu.VMEM((16 + tm, POOL_WIDTH), F32), pltpu.VMEM((tm, D_MODEL), F32)],
        compiler_params=pltpu.CompilerParams(dimension_semantics=("arbitrary",),
                                             vmem_limit_bytes=VMEM_LIMIT),
        name="premix",
    )(*args)


def _post_kernel(x1_ref, mix_ref, wo_ref, g2_ref, w2i_ref, w2o_ref, gf_ref, y_ref, acc_ref, *, final):
    x2 = x1_ref[...] + _dot(mix_ref[...], wo_ref[...])
    _run(_ffn_half_steps(x2, g2_ref[...], w2i_ref, w2o_ref, acc_ref))
    x3 = x2 + 0.5 * acc_ref[...]
    y_ref[...] = _rmsnorm(x3, gf_ref[...]) if final else x3


def _post_call(x1_rows, mix_rows, weights, gf, layer, final):
    m = x1_rows.shape[0]
    tm = ROW_TILE
    row = lambda w: pl.BlockSpec((tm, w), lambda i: (i, 0))
    return pl.pallas_call(
        functools.partial(_post_kernel, final=final),
        grid=(m // tm,),
        in_specs=[row(D_MODEL), row(D_MODEL), _layer_spec((D_MODEL, D_MODEL), layer),
                  _layer_spec((1, D_MODEL), layer), _layer_spec((D_MODEL, 2 * D_FF), layer),
                  _layer_spec((D_FF, D_MODEL), layer), _fixed_spec((1, D_MODEL))],
        out_specs=row(D_MODEL),
        out_shape=jax.ShapeDtypeStruct((m, D_MODEL), F32),
        scratch_shapes=[pltpu.VMEM((tm, D_MODEL), F32)],
        compiler_params=pltpu.CompilerParams(dimension_semantics=("arbitrary",),
                                             vmem_limit_bytes=VMEM_LIMIT),
        name="post",
    )(x1_rows, mix_rows, *weights, gf)


def _mix_kernel(q_ref, k_ref, lf_ref, v_ref, gt_ref, zp_ref, s0_ref, pp_ref, cnt_ref, hgn_ref, pw_ref, ps_ref,
                *rest, tt, chunk, nseq, t_real):
    mix_ref, sout_ref, pout_ref, st_scr, ubuf, mixf = rest[-6:]
    proj = (q_ref, k_ref, lf_ref, v_ref, gt_ref, zp_ref)
    consts = (cnt_ref, hgn_ref, pw_ref, ps_ref)

    def seq_step(n):
        _load_state(s0_ref, pp_ref, n, st_scr, ubuf)
        base = n * tt if isinstance(n, int) else pl.multiple_of(n * tt, SUBLANES)
        _run(_mix_sequence_tile_steps(proj, consts, mixf, st_scr, ubuf, base=base, tt=tt, chunk=chunk,
                                      t_real=t_real, unrolled=False))
        _store_state(sout_ref, pout_ref, n, st_scr, ubuf, t_real)

    if nseq == 1:
        seq_step(0)
    else:
        lax.fori_loop(0, nseq, lambda n, _: (seq_step(n), 0)[1], 0)
    mix_ref[...] = mixf[...].astype(BF16)


def _mix_call(proj, s0, s0_layer, pp, pp_layer, cnt, mix_consts, prev, *, layer, row_offset, n_seq_total,
              tt, chunk, nseq, t_real):
    m = proj[0].shape[0]
    assert n_seq_total % nseq == 0
    r = nseq * tt
    assert row_offset % r == 0
    blk0 = row_offset // r
    rows = lambda w: pl.BlockSpec((r, w), lambda bi: (blk0 + bi, 0))
    in_specs = ([rows(HG_WIDTH)] * N_PROJ
                + [pl.BlockSpec((None, nseq) + SDIMS, lambda bi: (s0_layer, bi, 0, 0, 0)),
                   pl.BlockSpec((None, nseq) + PDIMS, lambda bi: (pp_layer, bi, 0, 0))]
                + _mix_const_specs(cnt, layer))
    args = [*proj, s0, pp, cnt, *mix_consts]
    aliases = {}
    for out_idx, buf in enumerate(prev):
        if buf is not None:
            in_specs.append(pl.BlockSpec(memory_space=pl.ANY))
            aliases[len(args)] = out_idx
            args.append(buf)
    kern = functools.partial(_mix_kernel, tt=tt, chunk=chunk, nseq=nseq, t_real=t_real)
    return pl.pallas_call(
        kern,
        grid=(n_seq_total // nseq,),
        in_specs=in_specs,
        input_output_aliases=aliases,
        out_specs=[pl.BlockSpec((r, D_MODEL), lambda bi: (blk0 + bi, 0)),
                   pl.BlockSpec((None, nseq) + SDIMS, lambda bi: (layer, bi, 0, 0, 0)),
                   pl.BlockSpec((None, nseq) + PDIMS, lambda bi: (layer, bi, 0, 0))],
        out_shape=[jax.ShapeDtypeStruct((m, D_MODEL), BF16),
                   jax.ShapeDtypeStruct((DEPTH, n_seq_total) + SDIMS, F32),
                   jax.ShapeDtypeStruct((DEPTH, n_seq_total) + PDIMS, F32)],
        scratch_shapes=[pltpu.VMEM(SDIMS, F32), pltpu.VMEM((16 + tt, POOL_WIDTH), F32),
                        pltpu.VMEM((r, D_MODEL), F32)],
        compiler_params=pltpu.CompilerParams(dimension_semantics=("arbitrary",),
                                             vmem_limit_bytes=VMEM_LIMIT),
        name="mix",
    )(*args)


def _lower_bound_params(lb_logits):
    p = jax.nn.softmax(lb_logits.astype(F32), axis=0)
    cs = jnp.cumsum(p, axis=0)
    lb = cs - cs[0:1]
    rows = jnp.stack([1.0 - lb, jnp.log(lb), jnp.log1p(-lb)], axis=1)
    return jnp.pad(rows, ((0, 0), (0, SUBLANES - 3), (0, 0)))


def kernel(x_prompt, x_sample, state_hgrn, state_pool, meta, lb_logits, norm_ffn1, w_ffn1_in, w_ffn1_out,
           norm_mix, w_in, hg_norm, pool_w, pool_scale, w_out, norm_ffn2, w_ffn2_in, w_ffn2_out, norm_final):
    bp, sp, _ = x_prompt.shape
    bs, ss, _ = x_sample.shape
    ss_pad = SUBLANES
    n_prompt = bp * sp
    n_sample = bs * ss_pad
    n_meta = ROW_TILE
    n_side = n_sample + n_meta

    xp = x_prompt.reshape(n_prompt, D_MODEL)
    xs = jnp.pad(x_sample, ((0, 0), (0, ss_pad - ss), (0, 0))).reshape(n_sample, D_MODEL)
    xm = jnp.pad(meta.astype(F32), ((0, n_meta - N_META), (0, 0)))

    gain3 = lambda a: a.reshape(DEPTH, 1, -1)
    pre_w = (gain3(norm_ffn1), w_ffn1_in.astype(BF16), w_ffn1_out.astype(BF16), gain3(norm_mix),
             w_in.astype(BF16), _lower_bound_params(lb_logits))
    post_w = (w_out.astype(BF16), gain3(norm_ffn2), w_ffn2_in.astype(BF16), w_ffn2_out.astype(BF16))
    mix_consts = (hg_norm.reshape(DEPTH, 1, HG_WIDTH), pool_w.astype(BF16), gain3(pool_scale))
    gf = norm_final.reshape(1, D_MODEL)
    wmax = jnp.repeat(jnp.asarray(POOL_WINDOWS, F32), POOL_GC)[None, :]
    cnt_meta = jnp.minimum(wmax, jnp.arange(1, n_meta + 1, dtype=F32)[:, None])
    zero_s = jnp.zeros((1, 1) + SDIMS, F32)
    zero_p = jnp.zeros((1, 1) + PDIMS, F32)

    side_in = [xs, xm]
    st_m = st_p = st_s = (None, None)
    for l in range(DEPTH):
        last = l == DEPTH - 1
        x1_side, *proj = _pre_call(side_in, n_side, pre_w, l)
        mix_side, *st_m = _mix_call(proj, zero_s, 0, zero_p, 0, cnt_meta, mix_consts, (None, *st_m),
                                    layer=l, row_offset=n_sample, n_seq_total=1,
                                    tt=n_meta, chunk=MIX_CHUNK, nseq=1, t_real=N_META)
        x1, mix, *st_p = _premix_call(xp, pre_w, st_m[0], st_m[1], wmax, mix_consts,
                                      None if l == 0 else st_p, layer=l, seq_len=sp)
        mix_side, *st_s = _mix_call(proj, state_hgrn, l, state_pool, l, wmax, mix_consts, (mix_side, *st_s),
                                    layer=l, row_offset=0, n_seq_total=bs,
                                    tt=ss_pad, chunk=ss_pad, nseq=SAMPLE_SEQS, t_real=ss)
        xp = _post_call(x1, mix, post_w, gf, l, last)
        side_in = [_post_call(x1_side, mix_side, post_w, gf, l, last)]

    y_prompt = xp.reshape(bp, sp, D_MODEL)
    y_sample = side_in[0][:n_sample].reshape(bs, ss_pad, D_MODEL)[:, :ss]
    return (y_prompt, y_sample, st_p[0], st_p[1], st_s[0], st_s[1])
```

```python
import functools

import jax
import jax.numpy as jnp
from jax import lax
from jax.experimental import pallas as pl
from jax.experimental.pallas import tpu as pltpu

F32 = jnp.float32
BF16 = jnp.bfloat16

D_MODEL = 1024
DEPTH = 4
N_META = 16
HG_WIDTH = 512
HG_HEADS = 4
HG_D = 128
POOL_WIDTH = 512
POOL_WINDOWS = (2, 4, 8, 16)
POOL_GC = 128
POOL_STATE = 15
IN_COLS = 4 * HG_WIDTH + POOL_WIDTH
D_FF = 2816
EPS = 1e-6
N_PROJ = 6

SUBLANES = 8
MXU_DIM = 256
FF_CHUNK = MXU_DIM
ROW_TILE = 512
MIX_CHUNK = 64
SAMPLE_SEQS = 32
VMEM_LIMIT = 56 * 1024 * 1024
NEG_BIG = -1e30
LOG2E = 1.4426950408889634

SDIMS = (HG_HEADS, HG_D, HG_D)
PDIMS = (POOL_STATE, POOL_WIDTH)


def _layer_spec(shape, layer):
    nd = len(shape)
    return pl.BlockSpec((None,) + tuple(shape), lambda *_: (layer,) + (0,) * nd, pipeline_mode=pl.Buffered(1))


def _fixed_spec(shape):
    nd = len(shape)
    return pl.BlockSpec(tuple(shape), lambda *_: (0,) * nd)


def _rmsnorm(x, gain):
    return x * lax.rsqrt(jnp.mean(x * x, axis=-1, keepdims=True) + EPS) * gain


def _dot(a, b):
    return jnp.dot(a, b, preferred_element_type=F32)


def _exp_neg(z):
    return jnp.exp2(z * (-LOG2E))


def _silu(z):
    return z * (1.0 / (1.0 + _exp_neg(z)))


def _run(steps):
    for _ in steps:
        pass


def _interleave(steps_a, n_a, steps_b, n_b):
    done_b = 0
    for i, _ in enumerate(steps_a):
        while done_b < min(n_b, ((i + 1) * n_b) // n_a):
            next(steps_b)
            done_b += 1
    _run(steps_b)


def _ffn_half_steps(x, gain, win_ref, wout_ref, acc_ref):
    hn = _rmsnorm(x, gain).astype(BF16)
    for c in range(D_FF // FF_CHUNK):
        lo = c * FF_CHUNK
        gate = _dot(hn, win_ref[:, lo:lo + FF_CHUNK])
        up = _dot(hn, win_ref[:, D_FF + lo:D_FF + lo + FF_CHUNK])
        yield
        part = _dot((_silu(gate) * up).astype(BF16), wout_ref[lo:lo + FF_CHUNK, :])
        if c == 0:
            acc_ref[...] = part
        else:
            acc_ref[...] += part
        yield


N_FFN_STEPS = 2 * (D_FF // FF_CHUNK)
N_PRE_STEPS = N_FFN_STEPS + 5


def _pre_rows_steps(x, g1_ref, w1i_ref, w1o_ref, gm_ref, win_ref, lbp_ref, acc_ref, x1_ref, proj_refs):
    q_ref, k_ref, lf_ref, v_ref, gt_ref, zp_ref = proj_refs
    yield from _ffn_half_steps(x, g1_ref[...], w1i_ref, w1o_ref, acc_ref)
    x1 = x + 0.5 * acc_ref[...]
    x1_ref[...] = x1
    hn = _rmsnorm(x1, gm_ref[...]).astype(BF16)
    W = HG_WIDTH
    zf = _dot(hn, win_ref[:, W:2 * W])
    yield
    q_ref[...] = _silu(_dot(hn, win_ref[:, 0:W]))
    yield
    v_ref[...] = _dot(hn, win_ref[:, 2 * W:3 * W])
    yield
    gt_ref[...] = _silu(_dot(hn, win_ref[:, 3 * W:4 * W]))
    yield
    zp_ref[...] = _dot(hn, win_ref[:, 4 * W:5 * W])
    yield
    one_m_lb = lbp_ref[0:1, :]
    log_lb = lbp_ref[1:2, :]
    log1m_lb = lbp_ref[2:3, :]
    e = _exp_neg(jnp.abs(zf))
    k_ref[...] = one_m_lb * (jnp.where(zf >= 0.0, e, 1.0) * (1.0 / (1.0 + e)))
    c = log1m_lb + (jnp.minimum(zf, 0.0) - jnp.log(1.0 + e))
    lf_ref[...] = (jnp.maximum(log_lb, c) + jnp.log(1.0 + _exp_neg(jnp.abs(log_lb - c)))) * LOG2E


def _pre_weight_specs(layer):
    return [_layer_spec((1, D_MODEL), layer), _layer_spec((D_MODEL, 2 * D_FF), layer),
            _layer_spec((D_FF, D_MODEL), layer), _layer_spec((1, D_MODEL), layer),
            _layer_spec((D_MODEL, IN_COLS), layer), _layer_spec((SUBLANES, HG_WIDTH), layer)]


def _intra_block_operands(bh, qh, kh, L):
    D = HG_D
    row_l = lax.broadcasted_iota(jnp.int32, (L, D), 0)
    operands = []
    half = L // 2
    while half >= SUBLANES:
        pair = 2 * half
        pieces = [jnp.broadcast_to(bh[p * pair + half - 1:p * pair + half, :], (pair, D))
                  for p in range(L // pair)]
        bref = pieces[0] if len(pieces) == 1 else jnp.concatenate(pieces, axis=0)
        second = (row_l & half) != 0
        qt = (qh * jnp.exp2(jnp.where(second, bh - bref, NEG_BIG))).astype(BF16)
        kt = (kh * jnp.exp2(jnp.where(second, NEG_BIG, bref - bh))).astype(BF16)
        operands.append((qt, kt))
        half //= 2
    return operands


def _dot_nt(a, b):
    return lax.dot_general(a, b, (((1,), (1,)), ((), ())), preferred_element_type=F32)


def _sum_block_products(products, L):
    row_i = lax.broadcasted_iota(jnp.int32, (L, L), 0)
    col_i = lax.broadcasted_iota(jnp.int32, (L, L), 1)
    scores = jnp.zeros((L, L), F32)
    pair = L
    for blk in products:
        scores = scores + jnp.where((row_i & -pair) == (col_i & -pair), blk, 0.0)
        pair //= 2
    return scores


def _intra_diag_scores(bh, qh, kh, L):
    D = HG_D
    nb = L // SUBLANES
    b3 = bh.reshape(nb, SUBLANES, D)
    q3 = qh.reshape(nb, SUBLANES, D)
    k3 = kh.reshape(nb, SUBLANES, D)
    t_in = lax.broadcasted_iota(jnp.int32, (nb, SUBLANES, D), 1)
    lane = lax.broadcasted_iota(jnp.int32, (nb, SUBLANES, L), 2)
    blk0 = lax.broadcasted_iota(jnp.int32, (nb, SUBLANES, L), 0) * SUBLANES
    diag = jnp.zeros((nb, SUBLANES, L), F32)
    for s in range(SUBLANES):
        bs = b3[:, s:s + 1, :]
        ks = k3[:, s:s + 1, :]
        p = q3 * ks * jnp.exp2(jnp.where(t_in >= s, b3 - bs, NEG_BIG))
        col = jnp.sum(p, axis=-1, keepdims=True)
        diag = jnp.where(lane == blk0 + s, col, diag)
    return diag.reshape(L, L)


def _time_cumsum(lf, L):
    if L == SUBLANES:
        rows = [lf[0:1, :]]
        for r in range(1, L):
            rows.append(rows[-1] + lf[r:r + 1, :])
        return jnp.concatenate(rows, axis=0)
    tri = (lax.broadcasted_iota(jnp.int32, (L, L), 0) >= lax.broadcasted_iota(jnp.int32, (L, L), 1)).astype(BF16)
    l1 = lf.astype(BF16)
    r1 = lf - l1.astype(F32)
    l2 = r1.astype(BF16)
    l3 = (r1 - l2.astype(F32)).astype(BF16)
    return _dot(tri, l1) + _dot(tri, l2) + _dot(tri, l3)


def _mix_sequence_tile_steps(proj, consts, mixf, st_scr, ubuf, *, base, tt, chunk, t_real, unrolled):
    q_ref, k_ref, lf_ref, v_ref, gt_ref, zp_ref = proj
    cnt_ref, hgn_ref, pw_ref, ps_ref = consts
    D = HG_D
    L = chunk
    tr = tt if t_real is None else t_real
    n_chunks = pl.cdiv(tr, L)

    rows_t = pl.ds(base, tt)
    ubuf[16:16 + tt, :] = zp_ref[rows_t, :]
    for gi, w in enumerate(POOL_WINDOWS):
        ls = slice(gi * POOL_GC, (gi + 1) * POOL_GC)
        cur = ubuf[16:16 + tt, ls]
        acc = cur
        for j in range(1, w):
            acc = acc + ubuf[16 - j:16 - j + tt, ls]
        d = acc / cnt_ref[:, ls] - cur
        y = _dot(d.astype(BF16), pw_ref[gi]) * ps_ref[:, ls]
        mixf[rows_t, HG_WIDTH + gi * POOL_GC:HG_WIDTH + (gi + 1) * POOL_GC] = y
        yield

    def chunk_rows(c):
        r0 = base + c * L
        return pl.ds(r0 if isinstance(r0, int) else pl.multiple_of(r0, SUBLANES), L)

    def chunk_cumsum(c):
        lf = lf_ref[chunk_rows(c), :]
        if t_real is not None:
            lf = jnp.where((lax.broadcasted_iota(jnp.int32, lf.shape, 0) + c * L) < t_real, lf, 0.0)
        return _time_cumsum(lf, L)

    def chunk_steps(c, b, next_b):
        rows = chunk_rows(c)
        prepared = []
        for h in range(HG_HEADS):
            hs = slice(h * D, (h + 1) * D)
            bh = b[:, hs]
            qh = q_ref[rows, hs]
            kh = k_ref[rows, hs]
            if t_real is not None:
                kh = jnp.where((lax.broadcasted_iota(jnp.int32, (L, D), 0) + c * L) < t_real, kh, 0.0)
            prepared.append((hs, bh, qh, kh, _intra_block_operands(bh, qh, kh, L), (qh * jnp.exp2(bh)).astype(BF16)))
        yield
        heads = []
        for h, (hs, bh, qh, kh, operands, qe) in enumerate(prepared):
            products = [_dot_nt(qt, kt) for qt, kt in operands]
            st = st_scr[h]
            o_inter = _dot_nt(qe, st.astype(BF16))
            heads.append((hs, bh, kh, st, o_inter, products, _intra_diag_scores(bh, qh, kh, L)))
        if next_b is not None:
            next_b.append(chunk_cumsum(c + 1))
        yield
        results = []
        for hs, bh, kh, st, o_inter, products, diag in heads:
            vb = v_ref[rows, hs].astype(BF16)
            blast = bh[L - 1:L, :]
            scores = _sum_block_products(products, L) + diag
            o = o_inter + _dot(scores.astype(BF16), vb)
            kd = (kh * jnp.exp2(blast - bh)).astype(BF16)
            upd = lax.dot_general(vb, kd, (((0,), (0,)), ((), ())), preferred_element_type=F32)
            results.append((hs, st, blast, o, upd))
        yield
        for h, (hs, st, blast, o, upd) in enumerate(results):
            st_scr[h] = st * jnp.exp2(blast) + upd
            on = o * lax.rsqrt(jnp.mean(o * o, axis=-1, keepdims=True) + EPS) * hgn_ref[:, hs]
            mixf[rows, hs] = on * gt_ref[rows, hs]

    if unrolled or n_chunks == 1:
        b = chunk_cumsum(0)
        yield
        for c in range(n_chunks):
            next_b = [] if c + 1 < n_chunks else None
            yield from chunk_steps(c, b, next_b)
            b = next_b[0] if next_b else None
    else:
        lax.fori_loop(0, n_chunks, lambda c, _: (_run(chunk_steps(c, chunk_cumsum(c), None)), 0)[1], 0)
    if n_chunks * L < tt:
        mixf[pl.ds(base + n_chunks * L, tt - n_chunks * L), 0:HG_WIDTH] = jnp.zeros(
            (tt - n_chunks * L, HG_WIDTH), F32)


def _load_state(s0_ref, pp_ref, sidx, st_scr, ubuf):
    for h in range(HG_HEADS):
        st_scr[h] = s0_ref[sidx, h].T
    ubuf[1:1 + POOL_STATE, :] = pp_ref[sidx]


def _store_state(sout_ref, pout_ref, n, st_scr, ubuf, t_rows):
    pout_ref[n] = ubuf[t_rows + 1:t_rows + 1 + POOL_STATE, :]
    for h in range(HG_HEADS):
        sout_ref[n, h] = st_scr[h].T


def _mix_const_specs(cnt, layer):
    return [_fixed_spec(cnt.shape), _layer_spec((1, HG_WIDTH), layer),
            _layer_spec((len(POOL_WINDOWS), POOL_GC, POOL_GC), layer), _layer_spec((1, POOL_WIDTH), layer)]


def _pre_kernel(*refs, split):
    n_x = 2 if split else 1
    weights = refs[n_x:n_x + 6]
    x1_ref = refs[n_x + 6]
    proj_refs = refs[n_x + 7:n_x + 7 + N_PROJ]
    acc_ref = refs[-1]
    if split:
        x = jnp.where(pl.program_id(0) < split, refs[0][...], refs[1][...])
    else:
        x = refs[0][...]
    _run(_pre_rows_steps(x, *weights, acc_ref, x1_ref, proj_refs))


def _pre_call(xs, n_rows, weights, layer):
    tm = ROW_TILE
    row = lambda w: pl.BlockSpec((tm, w), lambda i: (i, 0))
    split = None
    x_specs = [row(D_MODEL)]
    if len(xs) == 2:
        split = xs[0].shape[0] // tm
        n2 = xs[1].shape[0] // tm
        x_specs = [pl.BlockSpec((tm, D_MODEL), lambda i: (jnp.minimum(i, split - 1), 0)),
                   pl.BlockSpec((tm, D_MODEL), lambda i: (jnp.clip(i - split, 0, n2 - 1), 0))]
    outs = [jax.ShapeDtypeStruct((n_rows, D_MODEL), F32)] + [jax.ShapeDtypeStruct((n_rows, HG_WIDTH), F32)] * N_PROJ
    return pl.pallas_call(
        functools.partial(_pre_kernel, split=split),
        grid=(n_rows // tm,),
        in_specs=x_specs + _pre_weight_specs(layer),
        out_specs=[row(D_MODEL)] + [row(HG_WIDTH)] * N_PROJ,
        out_shape=outs,
        scratch_shapes=[pltpu.VMEM((tm, D_MODEL), F32)],
        compiler_params=pltpu.CompilerParams(dimension_semantics=("arbitrary",),
                                             vmem_limit_bytes=VMEM_LIMIT),
        name="pre",
    )(*xs, *weights)


def _premix_kernel(x_ref, g1_ref, w1i_ref, w1o_ref, gm_ref, win_ref, lbp_ref,
                   s0_ref, pp_ref, cnt_ref, hgn_ref, pw_ref, ps_ref, *rest, tiles_per_seq):
    n_alias = len(rest) - 4 - (N_PROJ + 4)
    x1_ref, mix_ref, sout_ref, pout_ref = rest[n_alias:n_alias + 4]
    proj_scr = rest[n_alias + 4:n_alias + 4 + N_PROJ]
    acc_ref, st_scr, ubuf, mixf = rest[n_alias + 4 + N_PROJ:]
    tm = ROW_TILE
    s = pl.program_id(0)
    t = jnp.maximum(s - 1, 0)
    pos = lax.rem(t, tiles_per_seq)

    @pl.when(s == 0)
    def _():
        for r in proj_scr:
            r[...] = jnp.zeros(r.shape, F32)

    @pl.when(pos == 0)
    def _():
        _load_state(s0_ref, pp_ref, 0, st_scr, ubuf)

    slot_mix = lax.rem(s + 1, 2)
    slot_pre = lax.rem(s, 2)
    mix_steps = _mix_sequence_tile_steps([r.at[slot_mix] for r in proj_scr], (cnt_ref, hgn_ref, pw_ref, ps_ref),
                                         mixf, st_scr, ubuf, base=0, tt=tm, chunk=MIX_CHUNK, t_real=None,
                                         unrolled=True)
    pre_steps = _pre_rows_steps(x_ref[...], g1_ref, w1i_ref, w1o_ref, gm_ref, win_ref, lbp_ref, acc_ref, x1_ref,
                                [r.at[slot_pre] for r in proj_scr])
    n_mix_steps = len(POOL_WINDOWS) + 1 + 3 * (tm // MIX_CHUNK)
    _interleave(pre_steps, N_FFN_STEPS, mix_steps, n_mix_steps)
    mix_ref[...] = mixf[...].astype(BF16)

    @pl.when(jnp.logical_and(s > 0, pos == tiles_per_seq - 1))
    def _():
        _store_state(sout_ref, pout_ref, 0, st_scr, ubuf, tm)

    ubuf[0:16, :] = ubuf[tm:tm + 16, :]


def _premix_call(x_rows, weights, s0, pp, cnt, mix_consts, prev, *, layer, seq_len):
    m = x_rows.shape[0]
    tm = ROW_TILE
    n = m // tm
    tiles_per_seq = seq_len // tm
    n_seq = m // seq_len
    mix_tile = lambda s: jnp.maximum(s - 1, 0)
    in_specs = ([pl.BlockSpec((tm, D_MODEL), lambda s: (jnp.minimum(s, n - 1), 0))] + _pre_weight_specs(layer)
                + [pl.BlockSpec((None, 1) + SDIMS, lambda s: (layer, 0, 0, 0, 0)),
                   pl.BlockSpec((None, 1) + PDIMS, lambda s: (layer, 0, 0, 0))]
                + _mix_const_specs(cnt, layer))
    args = [x_rows, *weights, s0, pp, cnt, *mix_consts]
    aliases = {}
    for out_idx, buf in zip((2, 3), prev or ()):
        in_specs.append(pl.BlockSpec(memory_space=pl.ANY))
        aliases[len(args)] = out_idx
        args.append(buf)
    return pl.pallas_call(
        functools.partial(_premix_kernel, tiles_per_seq=tiles_per_seq),
        grid=(n + 1,),
        in_specs=in_specs,
        input_output_aliases=aliases,
        out_specs=[pl.BlockSpec((tm, D_MODEL), lambda s: (jnp.minimum(s, n - 1), 0)),
                   pl.BlockSpec((tm, D_MODEL), lambda s: (mix_tile(s), 0)),
                   pl.BlockSpec((None, 1) + SDIMS, lambda s: (layer, mix_tile(s) // tiles_per_seq, 0, 0, 0)),
                   pl.BlockSpec((None, 1) + PDIMS, lambda s: (layer, mix_tile(s) // tiles_per_seq, 0, 0))],
        out_shape=[jax.ShapeDtypeStruct((m, D_MODEL), F32), jax.ShapeDtypeStruct((m, D_MODEL), BF16),
                   jax.ShapeDtypeStruct((DEPTH, n_seq) + SDIMS, F32),
                   jax.ShapeDtypeStruct((DEPTH, n_seq) + PDIMS, F32)],
        scratch_shapes=[pltpu.VMEM((2, tm, HG_WIDTH), F32)] * N_PROJ
        + [pltpu.VMEM((tm, D_MODEL), F32), pltpu.VMEM(SDIMS, F32),
           pltpu.VMEM((16 + tm, POOL_WIDTH), F32), pltpu.VMEM((tm, D_MODEL), F32)],
        compiler_params=pltpu.CompilerParams(dimension_semantics=("arbitrary",),
                                             vmem_limit_bytes=VMEM_LIMIT),
        name="premix",
    )(*args)


def _post_kernel(x1_ref, mix_ref, wo_ref, g2_ref, w2i_ref, w2o_ref, gf_ref, y_ref, acc_ref, *, final):
    x2 = x1_ref[...] + _dot(mix_ref[...], wo_ref[...])
    _run(_ffn_half_steps(x2, g2_ref[...], w2i_ref, w2o_ref, acc_ref))
    x3 = x2 + 0.5 * acc_ref[...]
    y_ref[...] = _rmsnorm(x3, gf_ref[...]) if final else x3


def _post_call(x1_rows, mix_rows, weights, gf, layer, final):
    m = x1_rows.shape[0]
    tm = ROW_TILE
    row = lambda w: pl.BlockSpec((tm, w), lambda i: (i, 0))
    return pl.pallas_call(
        functools.partial(_post_kernel, final=final),
        grid=(m // tm,),
        in_specs=[row(D_MODEL), row(D_MODEL), _layer_spec((D_MODEL, D_MODEL), layer),
                  _layer_spec((1, D_MODEL), layer), _layer_spec((D_MODEL, 2 * D_FF), layer),
                  _layer_spec((D_FF, D_MODEL), layer), _fixed_spec((1, D_MODEL))],
        out_specs=row(D_MODEL),
        out_shape=jax.ShapeDtypeStruct((m, D_MODEL), F32),
        scratch_shapes=[pltpu.VMEM((tm, D_MODEL), F32)],
        compiler_params=pltpu.CompilerParams(dimension_semantics=("arbitrary",),
                                             vmem_limit_bytes=VMEM_LIMIT),
        name="post",
    )(x1_rows, mix_rows, *weights, gf)


def _mix_kernel(q_ref, k_ref, lf_ref, v_ref, gt_ref, zp_ref, s0_ref, pp_ref, cnt_ref, hgn_ref, pw_ref, ps_ref,
                *rest, tt, chunk, nseq, t_real):
    mix_ref, sout_ref, pout_ref, st_scr, ubuf, mixf = rest[-6:]
    proj = (q_ref, k_ref, lf_ref, v_ref, gt_ref, zp_ref)
    consts = (cnt_ref, hgn_ref, pw_ref, ps_ref)

    def seq_step(n):
        _load_state(s0_ref, pp_ref, n, st_scr, ubuf)
        base = n * tt if isinstance(n, int) else pl.multiple_of(n * tt, SUBLANES)
        _run(_mix_sequence_tile_steps(proj, consts, mixf, st_scr, ubuf, base=base, tt=tt, chunk=chunk,
                                      t_real=t_real, unrolled=False))
        _store_state(sout_ref, pout_ref, n, st_scr, ubuf, t_real)

    if nseq == 1:
        seq_step(0)
    else:
        lax.fori_loop(0, nseq, lambda n, _: (seq_step(n), 0)[1], 0)
    mix_ref[...] = mixf[...].astype(BF16)


def _mix_call(proj, s0, s0_layer, pp, pp_layer, cnt, mix_consts, prev, *, layer, row_offset, n_seq_total,
              tt, chunk, nseq, t_real):
    m = proj[0].shape[0]
    assert n_seq_total % nseq == 0
    r = nseq * tt
    assert row_offset % r == 0
    blk0 = row_offset // r
    rows = lambda w: pl.BlockSpec((r, w), lambda bi: (blk0 + bi, 0))
    in_specs = ([rows(HG_WIDTH)] * N_PROJ
                + [pl.BlockSpec((None, nseq) + SDIMS, lambda bi: (s0_layer, bi, 0, 0, 0)),
                   pl.BlockSpec((None, nseq) + PDIMS, lambda bi: (pp_layer, bi, 0, 0))]
                + _mix_const_specs(cnt, layer))
    args = [*proj, s0, pp, cnt, *mix_consts]
    aliases = {}
    for out_idx, buf in enumerate(prev):
        if buf is not None:
            in_specs.append(pl.BlockSpec(memory_space=pl.ANY))
            aliases[len(args)] = out_idx
            args.append(buf)
    kern = functools.partial(_mix_kernel, tt=tt, chunk=chunk, nseq=nseq, t_real=t_real)
    return pl.pallas_call(
        kern,
        grid=(n_seq_total // nseq,),
        in_specs=in_specs,
        input_output_aliases=aliases,
        out_specs=[pl.BlockSpec((r, D_MODEL), lambda bi: (blk0 + bi, 0)),
                   pl.BlockSpec((None, nseq) + SDIMS, lambda bi: (layer, bi, 0, 0, 0)),
                   pl.BlockSpec((None, nseq) + PDIMS, lambda bi: (layer, bi, 0, 0))],
        out_shape=[jax.ShapeDtypeStruct((m, D_MODEL), BF16),
                   jax.ShapeDtypeStruct((DEPTH, n_seq_total) + SDIMS, F32),
                   jax.ShapeDtypeStruct((DEPTH, n_seq_total) + PDIMS, F32)],
        scratch_shapes=[pltpu.VMEM(SDIMS, F32), pltpu.VMEM((16 + tt, POOL_WIDTH), F32),
                        pltpu.VMEM((r, D_MODEL), F32)],
        compiler_params=pltpu.CompilerParams(dimension_semantics=("arbitrary",),
                                             vmem_limit_bytes=VMEM_LIMIT),
        name="mix",
    )(*args)


def _lower_bound_params(lb_logits):
    p = jax.nn.softmax(lb_logits.astype(F32), axis=0)
    cs = jnp.cumsum(p, axis=0)
    lb = cs - cs[0:1]
    rows = jnp.stack([1.0 - lb, jnp.log(lb), jnp.log1p(-lb)], axis=1)
    return jnp.pad(rows, ((0, 0), (0, SUBLANES - 3), (0, 0)))


def kernel(x_prompt, x_sample, state_hgrn, state_pool, meta, lb_logits, norm_ffn1, w_ffn1_in, w_ffn1_out,
           norm_mix, w_in, hg_norm, pool_w, pool_scale, w_out, norm_ffn2, w_ffn2_in, w_ffn2_out, norm_final):
    bp, sp, _ = x_prompt.shape
    bs, ss, _ = x_sample.shape
    ss_pad = SUBLANES
    n_prompt = bp * sp
    n_sample = bs * ss_pad
    n_meta = ROW_TILE
    n_side = n_sample + n_meta

    xp = x_prompt.reshape(n_prompt, D_MODEL)
    xs = jnp.pad(x_sample, ((0, 0), (0, ss_pad - ss), (0, 0))).reshape(n_sample, D_MODEL)
    xm = jnp.pad(meta.astype(F32), ((0, n_meta - N_META), (0, 0)))

    gain3 = lambda a: a.reshape(DEPTH, 1, -1)
    pre_w = (gain3(norm_ffn1), w_ffn1_in.astype(BF16), w_ffn1_out.astype(BF16), gain3(norm_mix),
             w_in.astype(BF16), _lower_bound_params(lb_logits))
    post_w = (w_out.astype(BF16), gain3(norm_ffn2), w_ffn2_in.astype(BF16), w_ffn2_out.astype(BF16))
    mix_consts = (hg_norm.reshape(DEPTH, 1, HG_WIDTH), pool_w.astype(BF16), gain3(pool_scale))
    gf = norm_final.reshape(1, D_MODEL)
    wmax = jnp.repeat(jnp.asarray(POOL_WINDOWS, F32), POOL_GC)[None, :]
    cnt_meta = jnp.minimum(wmax, jnp.arange(1, n_meta + 1, dtype=F32)[:, None])
    zero_s = jnp.zeros((1, 1) + SDIMS, F32)
    zero_p = jnp.zeros((1, 1) + PDIMS, F32)

    side_in = [xs, xm]
    st_m = st_p = st_s = (None, None)
    for l in range(DEPTH):
        last = l == DEPTH - 1
        x1_side, *proj = _pre_call(side_in, n_side, pre_w, l)
        mix_side, *st_m = _mix_call(proj, zero_s, 0, zero_p, 0, cnt_meta, mix_consts, (None, *st_m),
                                    layer=l, row_offset=n_sample, n_seq_total=1,
                                    tt=n_meta, chunk=MIX_CHUNK, nseq=1, t_real=N_META)
        x1, mix, *st_p = _premix_call(xp, pre_w, st_m[0], st_m[1], wmax, mix_consts,
                                      None if l == 0 else st_p, layer=l, seq_len=sp)
        mix_side, *st_s = _mix_call(proj, state_hgrn, l, state_pool, l, wmax, mix_consts, (mix_side, *st_s),
                                    layer=l, row_offset=0, n_seq_total=bs,
                                    tt=ss_pad, chunk=ss_pad, nseq=SAMPLE_SEQS, t_real=ss)
        xp = _post_call(x1, mix, post_w, gf, l, last)
        side_in = [_post_call(x1_side, mix_side, post_w, gf, l, last)]

    y_prompt = xp.reshape(bp, sp, D_MODEL)
    y_sample = side_in[0][:n_sample].reshape(bs, ss_pad, D_MODEL)[:, :ss]
    return (y_prompt, y_sample, st_p[0], st_p[1], st_s[0], st_s[1])
```

```python
import functools

import jax
import jax.numpy as jnp
from jax import lax
from jax.experimental import pallas as pl
from jax.experimental.pallas import tpu as pltpu

F32 = jnp.float32
BF16 = jnp.bfloat16

D_MODEL = 1024
DEPTH = 4
N_META = 16
HG_WIDTH = 512
HG_HEADS = 4
HG_D = 128
POOL_WIDTH = 512
POOL_WINDOWS = (2, 4, 8, 16)
POOL_GC = 128
POOL_STATE = 15
IN_COLS = 4 * HG_WIDTH + POOL_WIDTH
D_FF = 2816
EPS = 1e-6
N_PROJ = 6

SUBLANES = 8
MXU_DIM = 256
FF_CHUNK = MXU_DIM
ROW_TILE = 512
MIX_CHUNK = 64
SAMPLE_SEQS = 32
SAMPLE_GROUP = 4
VMEM_LIMIT = 56 * 1024 * 1024
NEG_BIG = -1e30
LOG2E = 1.4426950408889634

SDIMS = (HG_HEADS, HG_D, HG_D)
PDIMS = (POOL_STATE, POOL_WIDTH)


def _layer_spec(shape, layer):
    nd = len(shape)
    return pl.BlockSpec((None,) + tuple(shape), lambda *_: (layer,) + (0,) * nd, pipeline_mode=pl.Buffered(1))


def _fixed_spec(shape):
    nd = len(shape)
    return pl.BlockSpec(tuple(shape), lambda *_: (0,) * nd)


def _rmsnorm(x, gain):
    return x * lax.rsqrt(jnp.mean(x * x, axis=-1, keepdims=True) + EPS) * gain


def _dot(a, b):
    return jnp.dot(a, b, preferred_element_type=F32)


def _exp_neg(z):
    return jnp.exp2(z * (-LOG2E))


def _silu(z):
    return z * (1.0 / (1.0 + _exp_neg(z)))


def _run(steps):
    for _ in steps:
        pass


def _interleave(steps_a, n_a, steps_b, n_b):
    done_b = 0
    for i, _ in enumerate(steps_a):
        while done_b < min(n_b, ((i + 1) * n_b) // n_a):
            next(steps_b)
            done_b += 1
    _run(steps_b)


def _ffn_half_steps(x, gain, win_ref, wout_ref, acc_ref):
    hn = _rmsnorm(x, gain).astype(BF16)
    for c in range(D_FF // FF_CHUNK):
        lo = c * FF_CHUNK
        gate = _dot(hn, win_ref[:, lo:lo + FF_CHUNK])
        up = _dot(hn, win_ref[:, D_FF + lo:D_FF + lo + FF_CHUNK])
        yield
        part = _dot((_silu(gate) * up).astype(BF16), wout_ref[lo:lo + FF_CHUNK, :])
        if c == 0:
            acc_ref[...] = part
        else:
            acc_ref[...] += part
        yield


N_FFN_STEPS = 2 * (D_FF // FF_CHUNK)
N_PRE_STEPS = N_FFN_STEPS + 5


def _pre_rows_steps(x, g1_ref, w1i_ref, w1o_ref, gm_ref, win_ref, lbp_ref, acc_ref, x1_ref, proj_refs):
    q_ref, k_ref, lf_ref, v_ref, gt_ref, zp_ref = proj_refs
    yield from _ffn_half_steps(x, g1_ref[...], w1i_ref, w1o_ref, acc_ref)
    x1 = x + 0.5 * acc_ref[...]
    x1_ref[...] = x1
    hn = _rmsnorm(x1, gm_ref[...]).astype(BF16)
    W = HG_WIDTH
    zf = _dot(hn, win_ref[:, W:2 * W])
    yield
    q_ref[...] = _silu(_dot(hn, win_ref[:, 0:W]))
    yield
    v_ref[...] = _dot(hn, win_ref[:, 2 * W:3 * W])
    yield
    gt_ref[...] = _silu(_dot(hn, win_ref[:, 3 * W:4 * W]))
    yield
    zp_ref[...] = _dot(hn, win_ref[:, 4 * W:5 * W])
    yield
    one_m_lb = lbp_ref[0:1, :]
    log_lb = lbp_ref[1:2, :]
    log1m_lb = lbp_ref[2:3, :]
    e = _exp_neg(jnp.abs(zf))
    k_ref[...] = one_m_lb * (jnp.where(zf >= 0.0, e, 1.0) * (1.0 / (1.0 + e)))
    c = log1m_lb + (jnp.minimum(zf, 0.0) - jnp.log(1.0 + e))
    lf_ref[...] = (jnp.maximum(log_lb, c) + jnp.log(1.0 + _exp_neg(jnp.abs(log_lb - c)))) * LOG2E


def _pre_weight_specs(layer):
    return [_layer_spec((1, D_MODEL), layer), _layer_spec((D_MODEL, 2 * D_FF), layer),
            _layer_spec((D_FF, D_MODEL), layer), _layer_spec((1, D_MODEL), layer),
            _layer_spec((D_MODEL, IN_COLS), layer), _layer_spec((SUBLANES, HG_WIDTH), layer)]


def _intra_block_operands(bh, qh, kh, L):
    D = HG_D
    row_l = lax.broadcasted_iota(jnp.int32, (L, D), 0)
    operands = []
    half = L // 2
    while half >= SUBLANES:
        pair = 2 * half
        pieces = [jnp.broadcast_to(bh[p * pair + half - 1:p * pair + half, :], (pair, D))
                  for p in range(L // pair)]
        bref = pieces[0] if len(pieces) == 1 else jnp.concatenate(pieces, axis=0)
        second = (row_l & half) != 0
        qt = (qh * jnp.exp2(jnp.where(second, bh - bref, NEG_BIG))).astype(BF16)
        kt = (kh * jnp.exp2(jnp.where(second, NEG_BIG, bref - bh))).astype(BF16)
        operands.append((qt, kt))
        half //= 2
    return operands


def _dot_nt(a, b):
    return lax.dot_general(a, b, (((1,), (1,)), ((), ())), preferred_element_type=F32)


def _sum_block_products(products, L):
    row_i = lax.broadcasted_iota(jnp.int32, (L, L), 0)
    col_i = lax.broadcasted_iota(jnp.int32, (L, L), 1)
    scores = jnp.zeros((L, L), F32)
    pair = L
    for blk in products:
        scores = scores + jnp.where((row_i & -pair) == (col_i & -pair), blk, 0.0)
        pair //= 2
    return scores


def _intra_diag_scores(bh, qh, kh, L):
    D = HG_D
    nb = L // SUBLANES
    b3 = bh.reshape(nb, SUBLANES, D)
    q3 = qh.reshape(nb, SUBLANES, D)
    k3 = kh.reshape(nb, SUBLANES, D)
    t_in = lax.broadcasted_iota(jnp.int32, (nb, SUBLANES, D), 1)
    lane = lax.broadcasted_iota(jnp.int32, (nb, SUBLANES, L), 2)
    blk0 = lax.broadcasted_iota(jnp.int32, (nb, SUBLANES, L), 0) * SUBLANES
    diag = jnp.zeros((nb, SUBLANES, L), F32)
    for s in range(SUBLANES):
        bs = b3[:, s:s + 1, :]
        ks = k3[:, s:s + 1, :]
        p = q3 * ks * jnp.exp2(jnp.where(t_in >= s, b3 - bs, NEG_BIG))
        col = jnp.sum(p, axis=-1, keepdims=True)
        diag = jnp.where(lane == blk0 + s, col, diag)
    return diag.reshape(L, L)


def _time_cumsum(lf, L):
    if L == SUBLANES:
        rows = [lf[0:1, :]]
        for r in range(1, L):
            rows.append(rows[-1] + lf[r:r + 1, :])
        return jnp.concatenate(rows, axis=0)
    tri = (lax.broadcasted_iota(jnp.int32, (L, L), 0) >= lax.broadcasted_iota(jnp.int32, (L, L), 1)).astype(BF16)
    l1 = lf.astype(BF16)
    r1 = lf - l1.astype(F32)
    l2 = r1.astype(BF16)
    l3 = (r1 - l2.astype(F32)).astype(BF16)
    return _dot(tri, l1) + _dot(tri, l2) + _dot(tri, l3)


def _mix_sequence_tile_steps(proj, consts, mixf, st_scr, ubuf, *, base, tt, chunk, t_real, unrolled):
    q_ref, k_ref, lf_ref, v_ref, gt_ref, zp_ref = proj
    cnt_ref, hgn_ref, pw_ref, ps_ref = consts
    D = HG_D
    L = chunk
    tr = tt if t_real is None else t_real
    n_chunks = pl.cdiv(tr, L)

    rows_t = pl.ds(base, tt)
    ubuf[16:16 + tt, :] = zp_ref[rows_t, :]
    for gi, w in enumerate(POOL_WINDOWS):
        ls = slice(gi * POOL_GC, (gi + 1) * POOL_GC)
        cur = ubuf[16:16 + tt, ls]
        acc = cur
        for j in range(1, w):
            acc = acc + ubuf[16 - j:16 - j + tt, ls]
        d = acc / cnt_ref[:, ls] - cur
        y = _dot(d.astype(BF16), pw_ref[gi]) * ps_ref[:, ls]
        mixf[rows_t, HG_WIDTH + gi * POOL_GC:HG_WIDTH + (gi + 1) * POOL_GC] = y
        yield

    def chunk_rows(c):
        r0 = base + c * L
        return pl.ds(r0 if isinstance(r0, int) else pl.multiple_of(r0, SUBLANES), L)

    def chunk_cumsum(c):
        lf = lf_ref[chunk_rows(c), :]
        if t_real is not None:
            lf = jnp.where((lax.broadcasted_iota(jnp.int32, lf.shape, 0) + c * L) < t_real, lf, 0.0)
        return _time_cumsum(lf, L)

    def chunk_steps(c, b, next_b):
        rows = chunk_rows(c)
        prepared = []
        for h in range(HG_HEADS):
            hs = slice(h * D, (h + 1) * D)
            bh = b[:, hs]
            qh = q_ref[rows, hs]
            kh = k_ref[rows, hs]
            if t_real is not None:
                kh = jnp.where((lax.broadcasted_iota(jnp.int32, (L, D), 0) + c * L) < t_real, kh, 0.0)
            prepared.append((hs, bh, qh, kh, _intra_block_operands(bh, qh, kh, L), (qh * jnp.exp2(bh)).astype(BF16)))
        yield
        heads = []
        for h, (hs, bh, qh, kh, operands, qe) in enumerate(prepared):
            products = [_dot_nt(qt, kt) for qt, kt in operands]
            st = st_scr[h]
            o_inter = _dot_nt(qe, st.astype(BF16))
            heads.append((hs, bh, kh, st, o_inter, products, _intra_diag_scores(bh, qh, kh, L)))
        if next_b is not None:
            next_b.append(chunk_cumsum(c + 1))
        yield
        results = []
        for hs, bh, kh, st, o_inter, products, diag in heads:
            vb = v_ref[rows, hs].astype(BF16)
            blast = bh[L - 1:L, :]
            scores = _sum_block_products(products, L) + diag
            o = o_inter + _dot(scores.astype(BF16), vb)
            kd = (kh * jnp.exp2(blast - bh)).astype(BF16)
            upd = lax.dot_general(vb, kd, (((0,), (0,)), ((), ())), preferred_element_type=F32)
            results.append((hs, st, blast, o, upd))
        yield
        for h, (hs, st, blast, o, upd) in enumerate(results):
            st_scr[h] = st * jnp.exp2(blast) + upd
            on = o * lax.rsqrt(jnp.mean(o * o, axis=-1, keepdims=True) + EPS) * hgn_ref[:, hs]
            mixf[rows, hs] = on * gt_ref[rows, hs]

    if unrolled or n_chunks == 1:
        b = chunk_cumsum(0)
        yield
        for c in range(n_chunks):
            next_b = [] if c + 1 < n_chunks else None
            yield from chunk_steps(c, b, next_b)
            b = next_b[0] if next_b else None
    else:
        lax.fori_loop(0, n_chunks, lambda c, _: (_run(chunk_steps(c, chunk_cumsum(c), None)), 0)[1], 0)
    if n_chunks * L < tt:
        mixf[pl.ds(base + n_chunks * L, tt - n_chunks * L), 0:HG_WIDTH] = jnp.zeros(
            (tt - n_chunks * L, HG_WIDTH), F32)


def _load_state(s0_ref, pp_ref, sidx, st_scr, ubuf):
    for h in range(HG_HEADS):
        st_scr[h] = s0_ref[sidx, h].T
    ubuf[1:1 + POOL_STATE, :] = pp_ref[sidx]


def _store_state(sout_ref, pout_ref, n, st_scr, ubuf, t_rows):
    pout_ref[n] = ubuf[t_rows + 1:t_rows + 1 + POOL_STATE, :]
    for h in range(HG_HEADS):
        sout_ref[n, h] = st_scr[h].T


def _mix_const_specs(cnt, layer):
    return [_fixed_spec(cnt.shape), _layer_spec((1, HG_WIDTH), layer),
            _layer_spec((len(POOL_WINDOWS), POOL_GC, POOL_GC), layer), _layer_spec((1, POOL_WIDTH), layer)]


def _pre_kernel(*refs, split):
    n_x = 2 if split else 1
    weights = refs[n_x:n_x + 6]
    x1_ref = refs[n_x + 6]
    proj_refs = refs[n_x + 7:n_x + 7 + N_PROJ]
    acc_ref = refs[-1]
    if split:
        x = jnp.where(pl.program_id(0) < split, refs[0][...], refs[1][...])
    else:
        x = refs[0][...]
    _run(_pre_rows_steps(x, *weights, acc_ref, x1_ref, proj_refs))


def _pre_call(xs, n_rows, weights, layer):
    tm = ROW_TILE
    row = lambda w: pl.BlockSpec((tm, w), lambda i: (i, 0))
    split = None
    x_specs = [row(D_MODEL)]
    if len(xs) == 2:
        split = xs[0].shape[0] // tm
        n2 = xs[1].shape[0] // tm
        x_specs = [pl.BlockSpec((tm, D_MODEL), lambda i: (jnp.minimum(i, split - 1), 0)),
                   pl.BlockSpec((tm, D_MODEL), lambda i: (jnp.clip(i - split, 0, n2 - 1), 0))]
    outs = [jax.ShapeDtypeStruct((n_rows, D_MODEL), F32)] + [jax.ShapeDtypeStruct((n_rows, HG_WIDTH), F32)] * N_PROJ
    return pl.pallas_call(
        functools.partial(_pre_kernel, split=split),
        grid=(n_rows // tm,),
        in_specs=x_specs + _pre_weight_specs(layer),
        out_specs=[row(D_MODEL)] + [row(HG_WIDTH)] * N_PROJ,
        out_shape=outs,
        scratch_shapes=[pltpu.VMEM((tm, D_MODEL), F32)],
        compiler_params=pltpu.CompilerParams(dimension_semantics=("arbitrary",),
                                             vmem_limit_bytes=VMEM_LIMIT),
        name="pre",
    )(*xs, *weights)


def _premix_kernel(x_ref, g1_ref, w1i_ref, w1o_ref, gm_ref, win_ref, lbp_ref,
                   s0_ref, pp_ref, cnt_ref, hgn_ref, pw_ref, ps_ref, *rest, tiles_per_seq):
    n_alias = len(rest) - 4 - (N_PROJ + 4)
    x1_ref, mix_ref, sout_ref, pout_ref = rest[n_alias:n_alias + 4]
    proj_scr = rest[n_alias + 4:n_alias + 4 + N_PROJ]
    acc_ref, st_scr, ubuf, mixf = rest[n_alias + 4 + N_PROJ:]
    tm = ROW_TILE
    s = pl.program_id(0)
    t = jnp.maximum(s - 1, 0)
    pos = lax.rem(t, tiles_per_seq)

    @pl.when(s == 0)
    def _():
        for r in proj_scr:
            r[...] = jnp.zeros(r.shape, F32)

    @pl.when(pos == 0)
    def _():
        _load_state(s0_ref, pp_ref, 0, st_scr, ubuf)

    slot_mix = lax.rem(s + 1, 2)
    slot_pre = lax.rem(s, 2)
    mix_steps = _mix_sequence_tile_steps([r.at[slot_mix] for r in proj_scr], (cnt_ref, hgn_ref, pw_ref, ps_ref),
                                         mixf, st_scr, ubuf, base=0, tt=tm, chunk=MIX_CHUNK, t_real=None,
                                         unrolled=True)
    pre_steps = _pre_rows_steps(x_ref[...], g1_ref, w1i_ref, w1o_ref, gm_ref, win_ref, lbp_ref, acc_ref, x1_ref,
                                [r.at[slot_pre] for r in proj_scr])
    n_mix_steps = len(POOL_WINDOWS) + 1 + 3 * (tm // MIX_CHUNK)
    _interleave(pre_steps, N_FFN_STEPS, mix_steps, n_mix_steps)
    mix_ref[...] = mixf[...].astype(BF16)

    @pl.when(jnp.logical_and(s > 0, pos == tiles_per_seq - 1))
    def _():
        _store_state(sout_ref, pout_ref, 0, st_scr, ubuf, tm)

    ubuf[0:16, :] = ubuf[tm:tm + 16, :]


def _premix_call(x_rows, weights, s0, pp, cnt, mix_consts, prev, *, layer, seq_len):
    m = x_rows.shape[0]
    tm = ROW_TILE
    n = m // tm
    tiles_per_seq = seq_len // tm
    n_seq = m // seq_len
    mix_tile = lambda s: jnp.maximum(s - 1, 0)
    in_specs = ([pl.BlockSpec((tm, D_MODEL), lambda s: (jnp.minimum(s, n - 1), 0))] + _pre_weight_specs(layer)
                + [pl.BlockSpec((None, 1) + SDIMS, lambda s: (layer, 0, 0, 0, 0)),
                   pl.BlockSpec((None, 1) + PDIMS, lambda s: (layer, 0, 0, 0))]
                + _mix_const_specs(cnt, layer))
    args = [x_rows, *weights, s0, pp, cnt, *mix_consts]
    aliases = {}
    for out_idx, buf in zip((2, 3), prev or ()):
        in_specs.append(pl.BlockSpec(memory_space=pl.ANY))
        aliases[len(args)] = out_idx
        args.append(buf)
    return pl.pallas_call(
        functools.partial(_premix_kernel, tiles_per_seq=tiles_per_seq),
        grid=(n + 1,),
        in_specs=in_specs,
        input_output_aliases=aliases,
        out_specs=[pl.BlockSpec((tm, D_MODEL), lambda s: (jnp.minimum(s, n - 1), 0)),
                   pl.BlockSpec((tm, D_MODEL), lambda s: (mix_tile(s), 0)),
                   pl.BlockSpec((None, 1) + SDIMS, lambda s: (layer, mix_tile(s) // tiles_per_seq, 0, 0, 0)),
                   pl.BlockSpec((None, 1) + PDIMS, lambda s: (layer, mix_tile(s) // tiles_per_seq, 0, 0))],
        out_shape=[jax.ShapeDtypeStruct((m, D_MODEL), F32), jax.ShapeDtypeStruct((m, D_MODEL), BF16),
                   jax.ShapeDtypeStruct((DEPTH, n_seq) + SDIMS, F32),
                   jax.ShapeDtypeStruct((DEPTH, n_seq) + PDIMS, F32)],
        scratch_shapes=[pltpu.VMEM((2, tm, HG_WIDTH), F32)] * N_PROJ
        + [pltpu.VMEM((tm, D_MODEL), F32), pltpu.VMEM(SDIMS, F32),
           pltpu.VMEM((16 + tm, POOL_WIDTH), F32), pltpu.VMEM((tm, D_MODEL), F32)],
        compiler_params=pltpu.CompilerParams(dimension_semantics=("arbitrary",),
                                             vmem_limit_bytes=VMEM_LIMIT),
        name="premix",
    )(*args)


def _post_kernel(x1_ref, mix_ref, wo_ref, g2_ref, w2i_ref, w2o_ref, gf_ref, y_ref, acc_ref, *, final):
    x2 = x1_ref[...] + _dot(mix_ref[...], wo_ref[...])
    _run(_ffn_half_steps(x2, g2_ref[...], w2i_ref, w2o_ref, acc_ref))
    x3 = x2 + 0.5 * acc_ref[...]
    y_ref[...] = _rmsnorm(x3, gf_ref[...]) if final else x3


def _post_call(x1_rows, mix_rows, weights, gf, layer, final):
    m = x1_rows.shape[0]
    tm = ROW_TILE
    row = lambda w: pl.BlockSpec((tm, w), lambda i: (i, 0))
    return pl.pallas_call(
        functools.partial(_post_kernel, final=final),
        grid=(m // tm,),
        in_specs=[row(D_MODEL), row(D_MODEL), _layer_spec((D_MODEL, D_MODEL), layer),
                  _layer_spec((1, D_MODEL), layer), _layer_spec((D_MODEL, 2 * D_FF), layer),
                  _layer_spec((D_FF, D_MODEL), layer), _fixed_spec((1, D_MODEL))],
        out_specs=row(D_MODEL),
        out_shape=jax.ShapeDtypeStruct((m, D_MODEL), F32),
        scratch_shapes=[pltpu.VMEM((tm, D_MODEL), F32)],
        compiler_params=pltpu.CompilerParams(dimension_semantics=("arbitrary",),
                                             vmem_limit_bytes=VMEM_LIMIT),
        name="post",
    )(x1_rows, mix_rows, *weights, gf)


def _split3_rows(e):
    e1 = e.astype(BF16).astype(F32)
    r1 = e - e1
    e2 = r1.astype(BF16).astype(F32)
    e3 = r1 - e2
    row = lax.broadcasted_iota(jnp.int32, (SUBLANES, e.shape[1]), 0)
    return jnp.where(row == 0, e1, jnp.where(row == 1, e2, jnp.where(row == 2, e3, 0.0))).astype(BF16)


def _dot_tn(a, b):
    return lax.dot_general(a, b, (((0,), (0,)), ((), ())), preferred_element_type=F32)


def _sample_mix_kernel(q_ref, k_ref, lf_ref, v_ref, gt_ref, zp_ref, s0_ref, pp_ref, cnt_ref, hgn_ref, pw_ref,
                       ps_ref, *rest, nseq, group, t_real):
    mix_ref, sout_ref, pout_ref, ubuf, dbuf, mixf = rest[-6:]
    D = HG_D
    T = SUBLANES
    row_w = lax.broadcasted_iota(jnp.int32, (T, HG_WIDTH), 0)
    row_d = lax.broadcasted_iota(jnp.int32, (T, D), 0)
    ones = jnp.ones((T, D), BF16)

    def group_step(g):
        prepared = []
        for j in range(group):
            n = g * group + j
            rows = pl.ds(pl.multiple_of(n * T, T), T)
            lf = jnp.where(row_w < t_real, lf_ref[rows, :], 0.0)
            b = lf
            for shift in (1, 2, 4):
                b = b + jnp.where(row_w >= shift, pltpu.roll(b, shift, 0), 0.0)
            for h in range(HG_HEADS):
                hs = slice(h * D, (h + 1) * D)
                bh = b[:, hs]
                qh = q_ref[rows, hs]
                kh = jnp.where(row_d < t_real, k_ref[rows, hs], 0.0)
                vh = v_ref[rows, hs]
                blast = bh[T - 1:T, :]
                qe = (qh * jnp.exp2(bh)).astype(BF16)
                kd = (kh * jnp.exp2(blast - bh)).astype(BF16)
                dec = _split3_rows(jnp.exp2(blast))
                o_intra = jnp.zeros((T, D), F32)
                for s in range(t_real):
                    p = qh * kh[s:s + 1, :] * jnp.exp2(jnp.where(row_d >= s, bh - bh[s:s + 1, :], NEG_BIG))
                    o_intra = o_intra + jnp.sum(p, axis=-1, keepdims=True) * vh[s:s + 1, :]
                prepared.append((n, rows, h, hs, qe, kd, vh.astype(BF16), dec, o_intra))
        products = []
        for n, rows, h, hs, qe, kd, vb, dec, o_intra in prepared:
            st = s0_ref[n, h]
            products.append((st, _dot(qe, st.astype(BF16)), _dot_tn(kd, vb), _dot_tn(dec, ones)))
        for (n, rows, h, hs, qe, kd, vb, dec, o_intra), (st, o_inter, upd, dec_rows) in zip(prepared, products):
            sout_ref[n, h] = st * dec_rows + upd
            o = o_inter + o_intra
            on = o * lax.rsqrt(jnp.mean(o * o, axis=-1, keepdims=True) + EPS) * hgn_ref[:, hs]
            mixf[rows, hs] = on * gt_ref[rows, hs]
        for j in range(group):
            n = g * group + j
            rows = pl.ds(pl.multiple_of(n * T, T), T)
            ubuf[j, 1:1 + POOL_STATE, :] = pp_ref[n]
            ubuf[j, 16:16 + T, :] = zp_ref[rows, :]
            for gi, w in enumerate(POOL_WINDOWS):
                ls = slice(gi * POOL_GC, (gi + 1) * POOL_GC)
                cur = ubuf[j, 16:16 + T, ls]
                acc = cur
                for jj in range(1, w):
                    acc = acc + ubuf[j, 16 - jj:16 - jj + T, ls]
                dbuf[rows, ls] = acc / cnt_ref[:, ls] - cur
            pout_ref[n] = ubuf[j, t_real + 1:t_real + 1 + POOL_STATE, :]

    lax.fori_loop(0, nseq // group, lambda g, _: (group_step(g), 0)[1], 0)
    for gi in range(len(POOL_WINDOWS)):
        ls = slice(gi * POOL_GC, (gi + 1) * POOL_GC)
        mixf[:, HG_WIDTH + gi * POOL_GC:HG_WIDTH + (gi + 1) * POOL_GC] = (
            _dot(dbuf[:, ls].astype(BF16), pw_ref[gi]) * ps_ref[:, ls])
    mix_ref[...] = mixf[...].astype(BF16)


def _sample_mix_call(proj, s0, pp, cnt, mix_consts, prev, *, layer, row_offset, n_seq_total, nseq, group, t_real):
    m = proj[0].shape[0]
    assert n_seq_total % nseq == 0 and nseq % group == 0
    r = nseq * SUBLANES
    assert row_offset % r == 0
    blk0 = row_offset // r
    rows = lambda w: pl.BlockSpec((r, w), lambda bi: (blk0 + bi, 0))
    in_specs = ([rows(HG_WIDTH)] * N_PROJ
                + [pl.BlockSpec((None, nseq) + SDIMS, lambda bi: (layer, bi, 0, 0, 0)),
                   pl.BlockSpec((None, nseq) + PDIMS, lambda bi: (layer, bi, 0, 0))]
                + _mix_const_specs(cnt, layer))
    args = [*proj, s0, pp, cnt, *mix_consts]
    aliases = {}
    for out_idx, buf in enumerate(prev):
        if buf is not None:
            in_specs.append(pl.BlockSpec(memory_space=pl.ANY))
            aliases[len(args)] = out_idx
            args.append(buf)
    return pl.pallas_call(
        functools.partial(_sample_mix_kernel, nseq=nseq, group=group, t_real=t_real),
        grid=(n_seq_total // nseq,),
        in_specs=in_specs,
        input_output_aliases=aliases,
        out_specs=[pl.BlockSpec((r, D_MODEL), lambda bi: (blk0 + bi, 0)),
                   pl.BlockSpec((None, nseq) + SDIMS, lambda bi: (layer, bi, 0, 0, 0)),
                   pl.BlockSpec((None, nseq) + PDIMS, lambda bi: (layer, bi, 0, 0))],
        out_shape=[jax.ShapeDtypeStruct((m, D_MODEL), BF16),
                   jax.ShapeDtypeStruct((DEPTH, n_seq_total) + SDIMS, F32),
                   jax.ShapeDtypeStruct((DEPTH, n_seq_total) + PDIMS, F32)],
        scratch_shapes=[pltpu.VMEM((group, 16 + SUBLANES, POOL_WIDTH), F32), pltpu.VMEM((r, POOL_WIDTH), F32),
                        pltpu.VMEM((r, D_MODEL), F32)],
        compiler_params=pltpu.CompilerParams(dimension_semantics=("arbitrary",),
                                             vmem_limit_bytes=VMEM_LIMIT),
        name="mix_sample",
    )(*args)


def _mix_kernel(q_ref, k_ref, lf_ref, v_ref, gt_ref, zp_ref, s0_ref, pp_ref, cnt_ref, hgn_ref, pw_ref, ps_ref,
                *rest, tt, chunk, nseq, t_real):
    mix_ref, sout_ref, pout_ref, st_scr, ubuf, mixf = rest[-6:]
    proj = (q_ref, k_ref, lf_ref, v_ref, gt_ref, zp_ref)
    consts = (cnt_ref, hgn_ref, pw_ref, ps_ref)

    def seq_step(n):
        _load_state(s0_ref, pp_ref, n, st_scr, ubuf)
        base = n * tt if isinstance(n, int) else pl.multiple_of(n * tt, SUBLANES)
        _run(_mix_sequence_tile_steps(proj, consts, mixf, st_scr, ubuf, base=base, tt=tt, chunk=chunk,
                                      t_real=t_real, unrolled=False))
        _store_state(sout_ref, pout_ref, n, st_scr, ubuf, t_real)

    if nseq == 1:
        seq_step(0)
    else:
        lax.fori_loop(0, nseq, lambda n, _: (seq_step(n), 0)[1], 0)
    mix_ref[...] = mixf[...].astype(BF16)


def _mix_call(proj, s0, s0_layer, pp, pp_layer, cnt, mix_consts, prev, *, layer, row_offset, n_seq_total,
              tt, chunk, nseq, t_real):
    m = proj[0].shape[0]
    assert n_seq_total % nseq == 0
    r = nseq * tt
    assert row_offset % r == 0
    blk0 = row_offset // r
    rows = lambda w: pl.BlockSpec((r, w), lambda bi: (blk0 + bi, 0))
    in_specs = ([rows(HG_WIDTH)] * N_PROJ
                + [pl.BlockSpec((None, nseq) + SDIMS, lambda bi: (s0_layer, bi, 0, 0, 0)),
                   pl.BlockSpec((None, nseq) + PDIMS, lambda bi: (pp_layer, bi, 0, 0))]
                + _mix_const_specs(cnt, layer))
    args = [*proj, s0, pp, cnt, *mix_consts]
    aliases = {}
    for out_idx, buf in enumerate(prev):
        if buf is not None:
            in_specs.append(pl.BlockSpec(memory_space=pl.ANY))
            aliases[len(args)] = out_idx
            args.append(buf)
    kern = functools.partial(_mix_kernel, tt=tt, chunk=chunk, nseq=nseq, t_real=t_real)
    return pl.pallas_call(
        kern,
        grid=(n_seq_total // nseq,),
        in_specs=in_specs,
        input_output_aliases=aliases,
        out_specs=[pl.BlockSpec((r, D_MODEL), lambda bi: (blk0 + bi, 0)),
                   pl.BlockSpec((None, nseq) + SDIMS, lambda bi: (layer, bi, 0, 0, 0)),
                   pl.BlockSpec((None, nseq) + PDIMS, lambda bi: (layer, bi, 0, 0))],
        out_shape=[jax.ShapeDtypeStruct((m, D_MODEL), BF16),
                   jax.ShapeDtypeStruct((DEPTH, n_seq_total) + SDIMS, F32),
                   jax.ShapeDtypeStruct((DEPTH, n_seq_total) + PDIMS, F32)],
        scratch_shapes=[pltpu.VMEM(SDIMS, F32), pltpu.VMEM((16 + tt, POOL_WIDTH), F32),
                        pltpu.VMEM((r, D_MODEL), F32)],
        compiler_params=pltpu.CompilerParams(dimension_semantics=("arbitrary",),
                                             vmem_limit_bytes=VMEM_LIMIT),
        name="mix",
    )(*args)


def _lower_bound_params(lb_logits):
    p = jax.nn.softmax(lb_logits.astype(F32), axis=0)
    cs = jnp.cumsum(p, axis=0)
    lb = cs - cs[0:1]
    rows = jnp.stack([1.0 - lb, jnp.log(lb), jnp.log1p(-lb)], axis=1)
    return jnp.pad(rows, ((0, 0), (0, SUBLANES - 3), (0, 0)))


def kernel(x_prompt, x_sample, state_hgrn, state_pool, meta, lb_logits, norm_ffn1, w_ffn1_in, w_ffn1_out,
           norm_mix, w_in, hg_norm, pool_w, pool_scale, w_out, norm_ffn2, w_ffn2_in, w_ffn2_out, norm_final):
    bp, sp, _ = x_prompt.shape
    bs, ss, _ = x_sample.shape
    ss_pad = SUBLANES
    n_prompt = bp * sp
    n_sample = bs * ss_pad
    n_meta = ROW_TILE
    n_side = n_sample + n_meta

    xp = x_prompt.reshape(n_prompt, D_MODEL)
    xs = jnp.pad(x_sample, ((0, 0), (0, ss_pad - ss), (0, 0))).reshape(n_sample, D_MODEL)
    xm = jnp.pad(meta.astype(F32), ((0, n_meta - N_META), (0, 0)))

    gain3 = lambda a: a.reshape(DEPTH, 1, -1)
    pre_w = (gain3(norm_ffn1), w_ffn1_in.astype(BF16), w_ffn1_out.astype(BF16), gain3(norm_mix),
             w_in.astype(BF16), _lower_bound_params(lb_logits))
    post_w = (w_out.astype(BF16), gain3(norm_ffn2), w_ffn2_in.astype(BF16), w_ffn2_out.astype(BF16))
    mix_consts = (hg_norm.reshape(DEPTH, 1, HG_WIDTH), pool_w.astype(BF16), gain3(pool_scale))
    gf = norm_final.reshape(1, D_MODEL)
    wmax = jnp.repeat(jnp.asarray(POOL_WINDOWS, F32), POOL_GC)[None, :]
    cnt_meta = jnp.minimum(wmax, jnp.arange(1, n_meta + 1, dtype=F32)[:, None])
    zero_s = jnp.zeros((1, 1) + SDIMS, F32)
    zero_p = jnp.zeros((1, 1) + PDIMS, F32)

    side_in = [xs, xm]
    st_m = st_p = st_s = (None, None)
    for l in range(DEPTH):
        last = l == DEPTH - 1
        x1_side, *proj = _pre_call(side_in, n_side, pre_w, l)
        mix_side, *st_m = _mix_call(proj, zero_s, 0, zero_p, 0, cnt_meta, mix_consts, (None, *st_m),
                                    layer=l, row_offset=n_sample, n_seq_total=1,
                                    tt=n_meta, chunk=MIX_CHUNK, nseq=1, t_real=N_META)
        x1, mix, *st_p = _premix_call(xp, pre_w, st_m[0], st_m[1], wmax, mix_consts,
                                      None if l == 0 else st_p, layer=l, seq_len=sp)
        mix_side, *st_s = _sample_mix_call(proj, state_hgrn, state_pool, wmax, mix_consts, (mix_side, *st_s),
                                           layer=l, row_offset=0, n_seq_total=bs, nseq=SAMPLE_SEQS,
                                           group=SAMPLE_GROUP, t_real=ss)
        xp = _post_call(x1, mix, post_w, gf, l, last)
        side_in = [_post_call(x1_side, mix_side, post_w, gf, l, last)]

    y_prompt = xp.reshape(bp, sp, D_MODEL)
    y_sample = side_in[0][:n_sample].reshape(bs, ss_pad, D_MODEL)[:, :ss]
    return (y_prompt, y_sample, st_p[0], st_p[1], st_s[0], st_s[1])
```

```python
import functools

import jax
import jax.numpy as jnp
from jax import lax
from jax.experimental import pallas as pl
from jax.experimental.pallas import tpu as pltpu

F32 = jnp.float32
BF16 = jnp.bfloat16

D_MODEL = 1024
DEPTH = 4
N_META = 16
HG_WIDTH = 512
HG_HEADS = 4
HG_D = 128
POOL_WIDTH = 512
POOL_WINDOWS = (2, 4, 8, 16)
POOL_GC = 128
POOL_STATE = 15
IN_COLS = 4 * HG_WIDTH + POOL_WIDTH
D_FF = 2816
EPS = 1e-6
N_PROJ = 6

SUBLANES = 8
MXU_DIM = 256
FF_CHUNK = MXU_DIM
ROW_TILE = 512
SIDE_TILE = 520
MIX_CHUNK = 64
SAMPLE_SEQS = 16
SAMPLE_GROUP = 4
VMEM_LIMIT = 56 * 1024 * 1024
NEG_BIG = -1e30
LOG2E = 1.4426950408889634

SDIMS = (HG_HEADS, HG_D, HG_D)
PDIMS = (POOL_STATE, POOL_WIDTH)


def _layer_spec(shape, layer):
    nd = len(shape)
    return pl.BlockSpec((None,) + tuple(shape), lambda *_: (layer,) + (0,) * nd, pipeline_mode=pl.Buffered(1))


def _fixed_spec(shape):
    nd = len(shape)
    return pl.BlockSpec(tuple(shape), lambda *_: (0,) * nd)


def _rmsnorm(x, gain):
    return x * lax.rsqrt(jnp.mean(x * x, axis=-1, keepdims=True) + EPS) * gain


def _dot(a, b):
    return jnp.dot(a, b, preferred_element_type=F32)


def _exp_neg(z):
    return jnp.exp2(z * (-LOG2E))


def _silu(z):
    return z * (1.0 / (1.0 + _exp_neg(z)))


def _run(steps):
    for _ in steps:
        pass


def _interleave(steps_a, n_a, steps_b, n_b):
    done_b = 0
    for i, _ in enumerate(steps_a):
        while done_b < min(n_b, ((i + 1) * n_b) // n_a):
            next(steps_b)
            done_b += 1
    _run(steps_b)


def _ffn_half_steps(x, gain, win_ref, wout_ref, acc_ref):
    hn = _rmsnorm(x, gain).astype(BF16)
    for c in range(D_FF // FF_CHUNK):
        lo = c * FF_CHUNK
        gate = _dot(hn, win_ref[:, lo:lo + FF_CHUNK])
        up = _dot(hn, win_ref[:, D_FF + lo:D_FF + lo + FF_CHUNK])
        yield
        part = _dot((_silu(gate) * up).astype(BF16), wout_ref[lo:lo + FF_CHUNK, :])
        if c == 0:
            acc_ref[...] = part
        else:
            acc_ref[...] += part
        yield


N_FFN_STEPS = 2 * (D_FF // FF_CHUNK)
N_PRE_STEPS = N_FFN_STEPS + 5


def _pre_rows_steps(x, g1_ref, w1i_ref, w1o_ref, gm_ref, win_ref, lbp_ref, acc_ref, x1_ref, proj_refs):
    q_ref, k_ref, lf_ref, v_ref, gt_ref, zp_ref = proj_refs
    yield from _ffn_half_steps(x, g1_ref[...], w1i_ref, w1o_ref, acc_ref)
    x1 = x + 0.5 * acc_ref[...]
    x1_ref[...] = x1
    hn = _rmsnorm(x1, gm_ref[...]).astype(BF16)
    W = HG_WIDTH
    zf = _dot(hn, win_ref[:, W:2 * W])
    yield
    q_ref[...] = _silu(_dot(hn, win_ref[:, 0:W]))
    yield
    v_ref[...] = _dot(hn, win_ref[:, 2 * W:3 * W])
    yield
    gt_ref[...] = _silu(_dot(hn, win_ref[:, 3 * W:4 * W]))
    yield
    zp_ref[...] = _dot(hn, win_ref[:, 4 * W:5 * W])
    yield
    one_m_lb = lbp_ref[0:1, :]
    log_lb = lbp_ref[1:2, :]
    log1m_lb = lbp_ref[2:3, :]
    e = _exp_neg(jnp.abs(zf))
    k_ref[...] = one_m_lb * (jnp.where(zf >= 0.0, e, 1.0) * (1.0 / (1.0 + e)))
    c = log1m_lb + (jnp.minimum(zf, 0.0) - jnp.log(1.0 + e))
    lf_ref[...] = (jnp.maximum(log_lb, c) + jnp.log(1.0 + _exp_neg(jnp.abs(log_lb - c)))) * LOG2E


def _pre_weight_specs(layer):
    return [_layer_spec((1, D_MODEL), layer), _layer_spec((D_MODEL, 2 * D_FF), layer),
            _layer_spec((D_FF, D_MODEL), layer), _layer_spec((1, D_MODEL), layer),
            _layer_spec((D_MODEL, IN_COLS), layer), _layer_spec((SUBLANES, HG_WIDTH), layer)]


def _intra_block_operands(bh, qh, kh, L):
    D = HG_D
    row_l = lax.broadcasted_iota(jnp.int32, (L, D), 0)
    operands = []
    half = L // 2
    while half >= SUBLANES:
        pair = 2 * half
        pieces = [jnp.broadcast_to(bh[p * pair + half - 1:p * pair + half, :], (pair, D))
                  for p in range(L // pair)]
        bref = pieces[0] if len(pieces) == 1 else jnp.concatenate(pieces, axis=0)
        second = (row_l & half) != 0
        qt = (qh * jnp.exp2(jnp.where(second, bh - bref, NEG_BIG))).astype(BF16)
        kt = (kh * jnp.exp2(jnp.where(second, NEG_BIG, bref - bh))).astype(BF16)
        operands.append((qt, kt))
        half //= 2
    return operands


def _dot_nt(a, b):
    return lax.dot_general(a, b, (((1,), (1,)), ((), ())), preferred_element_type=F32)


def _sum_block_products(products, L):
    row_i = lax.broadcasted_iota(jnp.int32, (L, L), 0)
    col_i = lax.broadcasted_iota(jnp.int32, (L, L), 1)
    scores = jnp.zeros((L, L), F32)
    pair = L
    for blk in products:
        scores = scores + jnp.where((row_i & -pair) == (col_i & -pair), blk, 0.0)
        pair //= 2
    return scores


def _intra_diag_scores(bh, qh, kh, L):
    D = HG_D
    nb = L // SUBLANES
    b3 = bh.reshape(nb, SUBLANES, D)
    q3 = qh.reshape(nb, SUBLANES, D)
    k3 = kh.reshape(nb, SUBLANES, D)
    t_in = lax.broadcasted_iota(jnp.int32, (nb, SUBLANES, D), 1)
    lane = lax.broadcasted_iota(jnp.int32, (nb, SUBLANES, L), 2)
    blk0 = lax.broadcasted_iota(jnp.int32, (nb, SUBLANES, L), 0) * SUBLANES
    diag = jnp.zeros((nb, SUBLANES, L), F32)
    for s in range(SUBLANES):
        bs = b3[:, s:s + 1, :]
        ks = k3[:, s:s + 1, :]
        p = q3 * ks * jnp.exp2(jnp.where(t_in >= s, b3 - bs, NEG_BIG))
        col = jnp.sum(p, axis=-1, keepdims=True)
        diag = jnp.where(lane == blk0 + s, col, diag)
    return diag.reshape(L, L)


def _time_cumsum(lf, L):
    if L == SUBLANES:
        rows = [lf[0:1, :]]
        for r in range(1, L):
            rows.append(rows[-1] + lf[r:r + 1, :])
        return jnp.concatenate(rows, axis=0)
    tri = (lax.broadcasted_iota(jnp.int32, (L, L), 0) >= lax.broadcasted_iota(jnp.int32, (L, L), 1)).astype(BF16)
    l1 = lf.astype(BF16)
    r1 = lf - l1.astype(F32)
    l2 = r1.astype(BF16)
    l3 = (r1 - l2.astype(F32)).astype(BF16)
    return _dot(tri, l1) + _dot(tri, l2) + _dot(tri, l3)


def _mix_sequence_tile_steps(proj, consts, mixf, st_scr, ubuf, *, base, tt, chunk, t_real, unrolled):
    q_ref, k_ref, lf_ref, v_ref, gt_ref, zp_ref = proj
    cnt_ref, hgn_ref, pw_ref, ps_ref = consts
    D = HG_D
    L = chunk
    tr = tt if t_real is None else t_real
    n_chunks = pl.cdiv(tr, L)

    rows_t = pl.ds(base, tt)
    ubuf[16:16 + tt, :] = zp_ref[rows_t, :]
    for gi, w in enumerate(POOL_WINDOWS):
        ls = slice(gi * POOL_GC, (gi + 1) * POOL_GC)
        cur = ubuf[16:16 + tt, ls]
        acc = cur
        for j in range(1, w):
            acc = acc + ubuf[16 - j:16 - j + tt, ls]
        d = acc / cnt_ref[:, ls] - cur
        y = _dot(d.astype(BF16), pw_ref[gi]) * ps_ref[:, ls]
        mixf[rows_t, HG_WIDTH + gi * POOL_GC:HG_WIDTH + (gi + 1) * POOL_GC] = y
        yield

    def chunk_rows(c):
        r0 = base + c * L
        return pl.ds(r0 if isinstance(r0, int) else pl.multiple_of(r0, SUBLANES), L)

    def chunk_cumsum(c):
        lf = lf_ref[chunk_rows(c), :]
        if t_real is not None:
            lf = jnp.where((lax.broadcasted_iota(jnp.int32, lf.shape, 0) + c * L) < t_real, lf, 0.0)
        return _time_cumsum(lf, L)

    def chunk_steps(c, b, next_b):
        rows = chunk_rows(c)
        prepared = []
        for h in range(HG_HEADS):
            hs = slice(h * D, (h + 1) * D)
            bh = b[:, hs]
            qh = q_ref[rows, hs]
            kh = k_ref[rows, hs]
            if t_real is not None:
                kh = jnp.where((lax.broadcasted_iota(jnp.int32, (L, D), 0) + c * L) < t_real, kh, 0.0)
            prepared.append((hs, bh, qh, kh, _intra_block_operands(bh, qh, kh, L), (qh * jnp.exp2(bh)).astype(BF16)))
        yield
        heads = []
        for h, (hs, bh, qh, kh, operands, qe) in enumerate(prepared):
            products = [_dot_nt(qt, kt) for qt, kt in operands]
            st = st_scr[h]
            o_inter = _dot_nt(qe, st.astype(BF16))
            heads.append((hs, bh, kh, st, o_inter, products, _intra_diag_scores(bh, qh, kh, L)))
        if next_b is not None:
            next_b.append(chunk_cumsum(c + 1))
        yield
        results = []
        for hs, bh, kh, st, o_inter, products, diag in heads:
            vb = v_ref[rows, hs].astype(BF16)
            blast = bh[L - 1:L, :]
            scores = _sum_block_products(products, L) + diag
            o = o_inter + _dot(scores.astype(BF16), vb)
            kd = (kh * jnp.exp2(blast - bh)).astype(BF16)
            upd = lax.dot_general(vb, kd, (((0,), (0,)), ((), ())), preferred_element_type=F32)
            results.append((hs, st, blast, o, upd))
        yield
        for h, (hs, st, blast, o, upd) in enumerate(results):
            st_scr[h] = st * jnp.exp2(blast) + upd
            on = o * lax.rsqrt(jnp.mean(o * o, axis=-1, keepdims=True) + EPS) * hgn_ref[:, hs]
            mixf[rows, hs] = on * gt_ref[rows, hs]

    if unrolled or n_chunks == 1:
        b = chunk_cumsum(0)
        yield
        for c in range(n_chunks):
            next_b = [] if c + 1 < n_chunks else None
            yield from chunk_steps(c, b, next_b)
            b = next_b[0] if next_b else None
    else:
        lax.fori_loop(0, n_chunks, lambda c, _: (_run(chunk_steps(c, chunk_cumsum(c), None)), 0)[1], 0)
    if n_chunks * L < tt:
        mixf[pl.ds(base + n_chunks * L, tt - n_chunks * L), 0:HG_WIDTH] = jnp.zeros(
            (tt - n_chunks * L, HG_WIDTH), F32)


def _load_state(s0_ref, pp_ref, sidx, st_scr, ubuf):
    for h in range(HG_HEADS):
        st_scr[h] = s0_ref[sidx, h].T
    ubuf[1:1 + POOL_STATE, :] = pp_ref[sidx]


def _store_state(sout_ref, pout_ref, n, st_scr, ubuf, t_rows):
    pout_ref[n] = ubuf[t_rows + 1:t_rows + 1 + POOL_STATE, :]
    for h in range(HG_HEADS):
        sout_ref[n, h] = st_scr[h].T


def _mix_const_specs(cnt, layer):
    return [_fixed_spec(cnt.shape), _layer_spec((1, HG_WIDTH), layer),
            _layer_spec((len(POOL_WINDOWS), POOL_GC, POOL_GC), layer), _layer_spec((1, POOL_WIDTH), layer)]


def _pre_kernel(x_ref, g1_ref, w1i_ref, w1o_ref, gm_ref, win_ref, lbp_ref, x1_ref, *rest):
    _run(_pre_rows_steps(x_ref[...], g1_ref, w1i_ref, w1o_ref, gm_ref, win_ref, lbp_ref, rest[-1], x1_ref,
                         rest[:N_PROJ]))


def _pre_call(x_rows, weights, layer, tm):
    n_rows = x_rows.shape[0]
    row = lambda w: pl.BlockSpec((tm, w), lambda i: (i, 0))
    outs = [jax.ShapeDtypeStruct((n_rows, D_MODEL), F32)] + [jax.ShapeDtypeStruct((n_rows, HG_WIDTH), F32)] * N_PROJ
    return pl.pallas_call(
        _pre_kernel,
        grid=(n_rows // tm,),
        in_specs=[row(D_MODEL)] + _pre_weight_specs(layer),
        out_specs=[row(D_MODEL)] + [row(HG_WIDTH)] * N_PROJ,
        out_shape=outs,
        scratch_shapes=[pltpu.VMEM((tm, D_MODEL), F32)],
        compiler_params=pltpu.CompilerParams(dimension_semantics=("arbitrary",),
                                             vmem_limit_bytes=VMEM_LIMIT),
        name="pre",
    )(x_rows, *weights)


def _premix_kernel(x_ref, g1_ref, w1i_ref, w1o_ref, gm_ref, win_ref, lbp_ref,
                   s0_ref, pp_ref, cnt_ref, hgn_ref, pw_ref, ps_ref, *rest, tiles_per_seq):
    n_alias = len(rest) - 4 - (N_PROJ + 4)
    x1_ref, mix_ref, sout_ref, pout_ref = rest[n_alias:n_alias + 4]
    proj_scr = rest[n_alias + 4:n_alias + 4 + N_PROJ]
    acc_ref, st_scr, ubuf, mixf = rest[n_alias + 4 + N_PROJ:]
    tm = ROW_TILE
    s = pl.program_id(0)
    t = jnp.maximum(s - 1, 0)
    pos = lax.rem(t, tiles_per_seq)

    @pl.when(s == 0)
    def _():
        for r in proj_scr:
            r[...] = jnp.zeros(r.shape, F32)

    @pl.when(pos == 0)
    def _():
        _load_state(s0_ref, pp_ref, 0, st_scr, ubuf)

    slot_mix = lax.rem(s + 1, 2)
    slot_pre = lax.rem(s, 2)
    mix_steps = _mix_sequence_tile_steps([r.at[slot_mix] for r in proj_scr], (cnt_ref, hgn_ref, pw_ref, ps_ref),
                                         mixf, st_scr, ubuf, base=0, tt=tm, chunk=MIX_CHUNK, t_real=None,
                                         unrolled=True)
    pre_steps = _pre_rows_steps(x_ref[...], g1_ref, w1i_ref, w1o_ref, gm_ref, win_ref, lbp_ref, acc_ref, x1_ref,
                                [r.at[slot_pre] for r in proj_scr])
    n_mix_steps = len(POOL_WINDOWS) + 1 + 3 * (tm // MIX_CHUNK)
    _interleave(pre_steps, N_FFN_STEPS, mix_steps, n_mix_steps)
    mix_ref[...] = mixf[...].astype(BF16)

    @pl.when(jnp.logical_and(s > 0, pos == tiles_per_seq - 1))
    def _():
        _store_state(sout_ref, pout_ref, 0, st_scr, ubuf, tm)

    ubuf[0:16, :] = ubuf[tm:tm + 16, :]


def _premix_call(x_rows, weights, s0, pp, cnt, mix_consts, prev, *, layer, seq_len):
    m = x_rows.shape[0]
    tm = ROW_TILE
    n = m // tm
    tiles_per_seq = seq_len // tm
    n_seq = m // seq_len
    mix_tile = lambda s: jnp.maximum(s - 1, 0)
    in_specs = ([pl.BlockSpec((tm, D_MODEL), lambda s: (jnp.minimum(s, n - 1), 0))] + _pre_weight_specs(layer)
                + [pl.BlockSpec((None, 1) + SDIMS, lambda s: (layer, 0, 0, 0, 0)),
                   pl.BlockSpec((None, 1) + PDIMS, lambda s: (layer, 0, 0, 0))]
                + _mix_const_specs(cnt, layer))
    args = [x_rows, *weights, s0, pp, cnt, *mix_consts]
    aliases = {}
    for out_idx, buf in zip((2, 3), prev or ()):
        in_specs.append(pl.BlockSpec(memory_space=pl.ANY))
        aliases[len(args)] = out_idx
        args.append(buf)
    return pl.pallas_call(
        functools.partial(_premix_kernel, tiles_per_seq=tiles_per_seq),
        grid=(n + 1,),
        in_specs=in_specs,
        input_output_aliases=aliases,
        out_specs=[pl.BlockSpec((tm, D_MODEL), lambda s: (jnp.minimum(s, n - 1), 0)),
                   pl.BlockSpec((tm, D_MODEL), lambda s: (mix_tile(s), 0)),
                   pl.BlockSpec((None, 1) + SDIMS, lambda s: (layer, mix_tile(s) // tiles_per_seq, 0, 0, 0)),
                   pl.BlockSpec((None, 1) + PDIMS, lambda s: (layer, mix_tile(s) // tiles_per_seq, 0, 0))],
        out_shape=[jax.ShapeDtypeStruct((m, D_MODEL), F32), jax.ShapeDtypeStruct((m, D_MODEL), BF16),
                   jax.ShapeDtypeStruct((DEPTH, n_seq) + SDIMS, F32),
                   jax.ShapeDtypeStruct((DEPTH, n_seq) + PDIMS, F32)],
        scratch_shapes=[pltpu.VMEM((2, tm, HG_WIDTH), F32)] * N_PROJ
        + [pltpu.VMEM((tm, D_MODEL), F32), pltpu.VMEM(SDIMS, F32),
           pltpu.VMEM((16 + tm, POOL_WIDTH), F32), pltpu.VMEM((tm, D_MODEL), F32)],
        compiler_params=pltpu.CompilerParams(dimension_semantics=("arbitrary",),
                                             vmem_limit_bytes=VMEM_LIMIT),
        name="premix",
    )(*args)


def _post_kernel(x1_ref, mix_ref, wo_ref, g2_ref, w2i_ref, w2o_ref, gf_ref, y_ref, acc_ref, *, final):
    x2 = x1_ref[...] + _dot(mix_ref[...].astype(BF16), wo_ref[...])
    _run(_ffn_half_steps(x2, g2_ref[...], w2i_ref, w2o_ref, acc_ref))
    x3 = x2 + 0.5 * acc_ref[...]
    y_ref[...] = _rmsnorm(x3, gf_ref[...]) if final else x3


def _post_call(x1_rows, mix_rows, weights, gf, layer, final, tm):
    m = x1_rows.shape[0]
    row = lambda w: pl.BlockSpec((tm, w), lambda i: (i, 0))
    return pl.pallas_call(
        functools.partial(_post_kernel, final=final),
        grid=(m // tm,),
        in_specs=[row(D_MODEL), row(D_MODEL), _layer_spec((D_MODEL, D_MODEL), layer),
                  _layer_spec((1, D_MODEL), layer), _layer_spec((D_MODEL, 2 * D_FF), layer),
                  _layer_spec((D_FF, D_MODEL), layer), _fixed_spec((1, D_MODEL))],
        out_specs=row(D_MODEL),
        out_shape=jax.ShapeDtypeStruct((m, D_MODEL), F32),
        scratch_shapes=[pltpu.VMEM((tm, D_MODEL), F32)],
        compiler_params=pltpu.CompilerParams(dimension_semantics=("arbitrary",),
                                             vmem_limit_bytes=VMEM_LIMIT),
        name="post",
    )(x1_rows, mix_rows, *weights, gf)


def _split3_rows(e):
    e1 = e.astype(BF16).astype(F32)
    r1 = e - e1
    e2 = r1.astype(BF16).astype(F32)
    e3 = r1 - e2
    row = lax.broadcasted_iota(jnp.int32, (SUBLANES, e.shape[1]), 0)
    return jnp.where(row == 0, e1, jnp.where(row == 1, e2, jnp.where(row == 2, e3, 0.0))).astype(BF16)


def _dot_tn(a, b):
    return lax.dot_general(a, b, (((0,), (0,)), ((), ())), preferred_element_type=F32)


def _sample_mix_kernel(q_ref, k_ref, lf_ref, v_ref, gt_ref, zp_ref, s0_ref, pp_ref, cnt_ref, hgn_ref, pw_ref,
                       ps_ref, *rest, nseq, group, t_real):
    mix_ref, sout_ref, pout_ref, ubuf, dbuf, mixf = rest[-6:]
    D = HG_D
    T = SUBLANES
    row_w = lax.broadcasted_iota(jnp.int32, (T, HG_WIDTH), 0)
    row_d = lax.broadcasted_iota(jnp.int32, (T, D), 0)
    ones = jnp.ones((T, D), BF16)

    def group_step(g):
        prepared = []
        for j in range(group):
            n = g * group + j
            rows = pl.ds(pl.multiple_of(n * T, T), T)
            lf = jnp.where(row_w < t_real, lf_ref[rows, :], 0.0)
            b = lf
            for shift in (1, 2, 4):
                b = b + jnp.where(row_w >= shift, pltpu.roll(b, shift, 0), 0.0)
            for h in range(HG_HEADS):
                hs = slice(h * D, (h + 1) * D)
                bh = b[:, hs]
                qh = q_ref[rows, hs]
                kh = jnp.where(row_d < t_real, k_ref[rows, hs], 0.0)
                vh = v_ref[rows, hs]
                blast = bh[T - 1:T, :]
                qe = (qh * jnp.exp2(bh)).astype(BF16)
                kd = (kh * jnp.exp2(blast - bh)).astype(BF16)
                dec = _split3_rows(jnp.exp2(blast))
                o_intra = jnp.zeros((T, D), F32)
                for s in range(t_real):
                    p = qh * kh[s:s + 1, :] * jnp.exp2(jnp.where(row_d >= s, bh - bh[s:s + 1, :], NEG_BIG))
                    o_intra = o_intra + jnp.sum(p, axis=-1, keepdims=True) * vh[s:s + 1, :]
                prepared.append((n, rows, h, hs, qe, kd, vh.astype(BF16), dec, o_intra))
        products = []
        for n, rows, h, hs, qe, kd, vb, dec, o_intra in prepared:
            st = s0_ref[n, h]
            products.append((st, _dot(qe, st.astype(BF16)), _dot_tn(kd, vb), _dot_tn(dec, ones)))
        for (n, rows, h, hs, qe, kd, vb, dec, o_intra), (st, o_inter, upd, dec_rows) in zip(prepared, products):
            sout_ref[n, h] = st * dec_rows + upd
            o = o_inter + o_intra
            on = o * lax.rsqrt(jnp.mean(o * o, axis=-1, keepdims=True) + EPS) * hgn_ref[:, hs]
            mixf[rows, hs] = on * gt_ref[rows, hs]
        for j in range(group):
            n = g * group + j
            rows = pl.ds(pl.multiple_of(n * T, T), T)
            ubuf[j, 1:1 + POOL_STATE, :] = pp_ref[n]
            ubuf[j, 16:16 + T, :] = zp_ref[rows, :]
            for gi, w in enumerate(POOL_WINDOWS):
                ls = slice(gi * POOL_GC, (gi + 1) * POOL_GC)
                cur = ubuf[j, 16:16 + T, ls]
                acc = cur
                for jj in range(1, w):
                    acc = acc + ubuf[j, 16 - jj:16 - jj + T, ls]
                dbuf[rows, ls] = acc / cnt_ref[:, ls] - cur
            pout_ref[n] = ubuf[j, t_real + 1:t_real + 1 + POOL_STATE, :]

    lax.fori_loop(0, nseq // group, lambda g, _: (group_step(g), 0)[1], 0)
    for gi in range(len(POOL_WINDOWS)):
        ls = slice(gi * POOL_GC, (gi + 1) * POOL_GC)
        mixf[:, HG_WIDTH + gi * POOL_GC:HG_WIDTH + (gi + 1) * POOL_GC] = (
            _dot(dbuf[:, ls].astype(BF16), pw_ref[gi]) * ps_ref[:, ls])
    mix_ref[...] = mixf[...].astype(mix_ref.dtype)


def _sample_mix_call(proj, s0, pp, cnt, mix_consts, prev, *, layer, row_offset, n_seq_total, nseq, group, t_real,
                     mix_dtype):
    m = proj[0].shape[0]
    assert n_seq_total % nseq == 0 and nseq % group == 0
    r = nseq * SUBLANES
    assert row_offset % r == 0
    blk0 = row_offset // r
    rows = lambda w: pl.BlockSpec((r, w), lambda bi: (blk0 + bi, 0))
    in_specs = ([rows(HG_WIDTH)] * N_PROJ
                + [pl.BlockSpec((None, nseq) + SDIMS, lambda bi: (layer, bi, 0, 0, 0)),
                   pl.BlockSpec((None, nseq) + PDIMS, lambda bi: (layer, bi, 0, 0))]
                + _mix_const_specs(cnt, layer))
    args = [*proj, s0, pp, cnt, *mix_consts]
    aliases = {}
    for out_idx, buf in enumerate(prev):
        if buf is not None:
            in_specs.append(pl.BlockSpec(memory_space=pl.ANY))
            aliases[len(args)] = out_idx
            args.append(buf)
    return pl.pallas_call(
        functools.partial(_sample_mix_kernel, nseq=nseq, group=group, t_real=t_real),
        grid=(n_seq_total // nseq,),
        in_specs=in_specs,
        input_output_aliases=aliases,
        out_specs=[pl.BlockSpec((r, D_MODEL), lambda bi: (blk0 + bi, 0)),
                   pl.BlockSpec((None, nseq) + SDIMS, lambda bi: (layer, bi, 0, 0, 0)),
                   pl.BlockSpec((None, nseq) + PDIMS, lambda bi: (layer, bi, 0, 0))],
        out_shape=[jax.ShapeDtypeStruct((m, D_MODEL), mix_dtype),
                   jax.ShapeDtypeStruct((DEPTH, n_seq_total) + SDIMS, F32),
                   jax.ShapeDtypeStruct((DEPTH, n_seq_total) + PDIMS, F32)],
        scratch_shapes=[pltpu.VMEM((group, 16 + SUBLANES, POOL_WIDTH), F32), pltpu.VMEM((r, POOL_WIDTH), F32),
                        pltpu.VMEM((r, D_MODEL), F32)],
        compiler_params=pltpu.CompilerParams(dimension_semantics=("arbitrary",),
                                             vmem_limit_bytes=VMEM_LIMIT),
        name="mix_sample",
    )(*args)


def _mix_kernel(q_ref, k_ref, lf_ref, v_ref, gt_ref, zp_ref, s0_ref, pp_ref, cnt_ref, hgn_ref, pw_ref, ps_ref,
                *rest, tt, chunk, nseq, t_real):
    mix_ref, sout_ref, pout_ref, st_scr, ubuf, mixf = rest[-6:]
    proj = (q_ref, k_ref, lf_ref, v_ref, gt_ref, zp_ref)
    consts = (cnt_ref, hgn_ref, pw_ref, ps_ref)

    def seq_step(n):
        _load_state(s0_ref, pp_ref, n, st_scr, ubuf)
        base = n * tt if isinstance(n, int) else pl.multiple_of(n * tt, SUBLANES)
        _run(_mix_sequence_tile_steps(proj, consts, mixf, st_scr, ubuf, base=base, tt=tt, chunk=chunk,
                                      t_real=t_real, unrolled=False))
        _store_state(sout_ref, pout_ref, n, st_scr, ubuf, t_real)

    if nseq == 1:
        seq_step(0)
    else:
        lax.fori_loop(0, nseq, lambda n, _: (seq_step(n), 0)[1], 0)
    mix_ref[...] = mixf[...].astype(mix_ref.dtype)


def _mix_call(proj, s0, s0_layer, pp, pp_layer, cnt, mix_consts, prev, *, layer, row_offset, n_seq_total,
              tt, chunk, nseq, t_real, mix_dtype):
    m = proj[0].shape[0]
    assert n_seq_total % nseq == 0
    r = nseq * tt
    assert row_offset % r == 0
    blk0 = row_offset // r
    rows = lambda w: pl.BlockSpec((r, w), lambda bi: (blk0 + bi, 0))
    in_specs = ([rows(HG_WIDTH)] * N_PROJ
                + [pl.BlockSpec((None, nseq) + SDIMS, lambda bi: (s0_layer, bi, 0, 0, 0)),
                   pl.BlockSpec((None, nseq) + PDIMS, lambda bi: (pp_layer, bi, 0, 0))]
                + _mix_const_specs(cnt, layer))
    args = [*proj, s0, pp, cnt, *mix_consts]
    aliases = {}
    for out_idx, buf in enumerate(prev):
        if buf is not None:
            in_specs.append(pl.BlockSpec(memory_space=pl.ANY))
            aliases[len(args)] = out_idx
            args.append(buf)
    kern = functools.partial(_mix_kernel, tt=tt, chunk=chunk, nseq=nseq, t_real=t_real)
    return pl.pallas_call(
        kern,
        grid=(n_seq_total // nseq,),
        in_specs=in_specs,
        input_output_aliases=aliases,
        out_specs=[pl.BlockSpec((r, D_MODEL), lambda bi: (blk0 + bi, 0)),
                   pl.BlockSpec((None, nseq) + SDIMS, lambda bi: (layer, bi, 0, 0, 0)),
                   pl.BlockSpec((None, nseq) + PDIMS, lambda bi: (layer, bi, 0, 0))],
        out_shape=[jax.ShapeDtypeStruct((m, D_MODEL), mix_dtype),
                   jax.ShapeDtypeStruct((DEPTH, n_seq_total) + SDIMS, F32),
                   jax.ShapeDtypeStruct((DEPTH, n_seq_total) + PDIMS, F32)],
        scratch_shapes=[pltpu.VMEM(SDIMS, F32), pltpu.VMEM((16 + tt, POOL_WIDTH), F32),
                        pltpu.VMEM((r, D_MODEL), F32)],
        compiler_params=pltpu.CompilerParams(dimension_semantics=("arbitrary",),
                                             vmem_limit_bytes=VMEM_LIMIT),
        name="mix",
    )(*args)


def _lower_bound_params(lb_logits):
    p = jax.nn.softmax(lb_logits.astype(F32), axis=0)
    cs = jnp.cumsum(p, axis=0)
    lb = cs - cs[0:1]
    rows = jnp.stack([1.0 - lb, jnp.log(lb), jnp.log1p(-lb)], axis=1)
    return jnp.pad(rows, ((0, 0), (0, SUBLANES - 3), (0, 0)))


def kernel(x_prompt, x_sample, state_hgrn, state_pool, meta, lb_logits, norm_ffn1, w_ffn1_in, w_ffn1_out,
           norm_mix, w_in, hg_norm, pool_w, pool_scale, w_out, norm_ffn2, w_ffn2_in, w_ffn2_out, norm_final):
    bp, sp, _ = x_prompt.shape
    bs, ss, _ = x_sample.shape
    ss_pad = SUBLANES
    n_prompt = bp * sp
    n_sample = bs * ss_pad
    xp = x_prompt.reshape(n_prompt, D_MODEL)
    x_side = jnp.concatenate([jnp.pad(x_sample, ((0, 0), (0, ss_pad - ss), (0, 0))).reshape(n_sample, D_MODEL),
                              meta.astype(F32)], axis=0)

    gain3 = lambda a: a.reshape(DEPTH, 1, -1)
    pre_w = (gain3(norm_ffn1), w_ffn1_in.astype(BF16), w_ffn1_out.astype(BF16), gain3(norm_mix),
             w_in.astype(BF16), _lower_bound_params(lb_logits))
    post_w = (w_out.astype(BF16), gain3(norm_ffn2), w_ffn2_in.astype(BF16), w_ffn2_out.astype(BF16))
    mix_consts = (hg_norm.reshape(DEPTH, 1, HG_WIDTH), pool_w.astype(BF16), gain3(pool_scale))
    gf = norm_final.reshape(1, D_MODEL)
    wmax = jnp.repeat(jnp.asarray(POOL_WINDOWS, F32), POOL_GC)[None, :]
    cnt_meta = jnp.minimum(wmax, jnp.arange(1, N_META + 1, dtype=F32)[:, None])
    zero_s = jnp.zeros((1, 1) + SDIMS, F32)
    zero_p = jnp.zeros((1, 1) + PDIMS, F32)

    st_m = st_p = st_s = (None, None)
    for l in range(DEPTH):
        last = l == DEPTH - 1
        x1_side, *proj = _pre_call(x_side, pre_w, l, SIDE_TILE)
        mix_side, *st_m = _mix_call(proj, zero_s, 0, zero_p, 0, cnt_meta, mix_consts, (None, *st_m),
                                    layer=l, row_offset=n_sample, n_seq_total=1,
                                    tt=N_META, chunk=SUBLANES, nseq=1, t_real=N_META, mix_dtype=F32)
        x1, mix, *st_p = _premix_call(xp, pre_w, st_m[0], st_m[1], wmax, mix_consts,
                                      None if l == 0 else st_p, layer=l, seq_len=sp)
        mix_side, *st_s = _sample_mix_call(proj, state_hgrn, state_pool, wmax, mix_consts, (mix_side, *st_s),
                                           layer=l, row_offset=0, n_seq_total=bs, nseq=SAMPLE_SEQS,
                                           group=SAMPLE_GROUP, t_real=ss, mix_dtype=F32)
        xp = _post_call(x1, mix, post_w, gf, l, last, ROW_TILE)
        x_side = _post_call(x1_side, mix_side, post_w, gf, l, last, SIDE_TILE)

    y_prompt = xp.reshape(bp, sp, D_MODEL)
    y_sample = x_side[:n_sample].reshape(bs, ss_pad, D_MODEL)[:, :ss]
    return (y_prompt, y_sample, st_p[0], st_p[1], st_s[0], st_s[1])
```

```python
import functools

import jax
import jax.numpy as jnp
from jax import lax
from jax.experimental import pallas as pl
from jax.experimental.pallas import tpu as pltpu

F32 = jnp.float32
BF16 = jnp.bfloat16

D_MODEL = 1024
DEPTH = 4
N_META = 16
HG_WIDTH = 512
HG_HEADS = 4
HG_D = 128
POOL_WIDTH = 512
POOL_WINDOWS = (2, 4, 8, 16)
POOL_GC = 128
POOL_STATE = 15
IN_COLS = 4 * HG_WIDTH + POOL_WIDTH
D_FF = 2816
EPS = 1e-6
N_PROJ = 6

SUBLANES = 8
MXU_DIM = 256
FF_CHUNK = MXU_DIM
ROW_TILE = 512
SIDE_TILE = 520
MIX_CHUNK = 64
SAMPLE_SEQS = 16
SAMPLE_GROUP = 4
VMEM_LIMIT = 56 * 1024 * 1024
NEG_BIG = -1e30
LOG2E = 1.4426950408889634

SDIMS = (HG_HEADS, HG_D, HG_D)
PDIMS = (POOL_STATE, POOL_WIDTH)


def _layer_spec(shape, layer):
    nd = len(shape)
    return pl.BlockSpec((None,) + tuple(shape), lambda *_: (layer,) + (0,) * nd, pipeline_mode=pl.Buffered(1))


def _fixed_spec(shape):
    nd = len(shape)
    return pl.BlockSpec(tuple(shape), lambda *_: (0,) * nd)


def _param_spec(arr, layer):
    if arr.ndim == 3:
        return _layer_spec(arr.shape[1:], layer)
    return pl.BlockSpec(arr.shape, lambda *_: (0, 0), pipeline_mode=pl.Buffered(1))


def _rmsnorm(x, gain):
    return x * lax.rsqrt(jnp.mean(x * x, axis=-1, keepdims=True) + EPS) * gain


def _dot(a, b):
    return jnp.dot(a, b, preferred_element_type=F32)


def _exp_neg(z):
    return jnp.exp2(z * (-LOG2E))


def _silu(z):
    return z * (1.0 / (1.0 + _exp_neg(z)))


def _run(steps):
    for _ in steps:
        pass


def _interleave(steps_a, n_a, steps_b, n_b):
    done_b = 0
    for i, _ in enumerate(steps_a):
        while done_b < min(n_b, ((i + 1) * n_b) // n_a):
            next(steps_b)
            done_b += 1
    _run(steps_b)


def _ffn_half_steps(x, gain, win_ref, wout_ref, acc_ref):
    hn = _rmsnorm(x, gain).astype(BF16)
    for c in range(D_FF // FF_CHUNK):
        lo = c * FF_CHUNK
        gate = _dot(hn, win_ref[:, lo:lo + FF_CHUNK])
        up = _dot(hn, win_ref[:, D_FF + lo:D_FF + lo + FF_CHUNK])
        yield
        part = _dot((_silu(gate) * up).astype(BF16), wout_ref[lo:lo + FF_CHUNK, :])
        if c == 0:
            acc_ref[...] = part
        else:
            acc_ref[...] += part
        yield


N_FFN_STEPS = 2 * (D_FF // FF_CHUNK)
N_PRE_STEPS = N_FFN_STEPS + 5


def _pre_rows_steps(x, g1_ref, w1i_ref, w1o_ref, gm_ref, win_ref, lbp_ref, acc_ref, x1_ref, proj_refs):
    q_ref, k_ref, lf_ref, v_ref, gt_ref, zp_ref = proj_refs
    yield from _ffn_half_steps(x, g1_ref[...], w1i_ref, w1o_ref, acc_ref)
    x1 = x + 0.5 * acc_ref[...]
    x1_ref[...] = x1
    hn = _rmsnorm(x1, gm_ref[...]).astype(BF16)
    W = HG_WIDTH
    zf = _dot(hn, win_ref[:, W:2 * W])
    yield
    q_ref[...] = _silu(_dot(hn, win_ref[:, 0:W]))
    yield
    v_ref[...] = _dot(hn, win_ref[:, 2 * W:3 * W])
    yield
    gt_ref[...] = _silu(_dot(hn, win_ref[:, 3 * W:4 * W]))
    yield
    zp_ref[...] = _dot(hn, win_ref[:, 4 * W:5 * W])
    yield
    one_m_lb = lbp_ref[0:1, :]
    log_lb = lbp_ref[1:2, :]
    log1m_lb = lbp_ref[2:3, :]
    e = _exp_neg(jnp.abs(zf))
    k_ref[...] = one_m_lb * (jnp.where(zf >= 0.0, e, 1.0) * (1.0 / (1.0 + e)))
    c = log1m_lb + (jnp.minimum(zf, 0.0) - jnp.log(1.0 + e))
    lf_ref[...] = (jnp.maximum(log_lb, c) + jnp.log(1.0 + _exp_neg(jnp.abs(log_lb - c)))) * LOG2E


def _pre_weight_specs(weights, layer):
    return [_param_spec(w, layer) for w in weights]


def _intra_block_operands(bh, qh, kh, L):
    D = HG_D
    row_l = lax.broadcasted_iota(jnp.int32, (L, D), 0)
    operands = []
    half = L // 2
    while half >= SUBLANES:
        pair = 2 * half
        pieces = [jnp.broadcast_to(bh[p * pair + half - 1:p * pair + half, :], (pair, D))
                  for p in range(L // pair)]
        bref = pieces[0] if len(pieces) == 1 else jnp.concatenate(pieces, axis=0)
        second = (row_l & half) != 0
        qt = (qh * jnp.exp2(jnp.where(second, bh - bref, NEG_BIG))).astype(BF16)
        kt = (kh * jnp.exp2(jnp.where(second, NEG_BIG, bref - bh))).astype(BF16)
        operands.append((qt, kt))
        half //= 2
    return operands


def _dot_nt(a, b):
    return lax.dot_general(a, b, (((1,), (1,)), ((), ())), preferred_element_type=F32)


def _sum_block_products(products, L):
    row_i = lax.broadcasted_iota(jnp.int32, (L, L), 0)
    col_i = lax.broadcasted_iota(jnp.int32, (L, L), 1)
    scores = jnp.zeros((L, L), F32)
    pair = L
    for blk in products:
        scores = scores + jnp.where((row_i & -pair) == (col_i & -pair), blk, 0.0)
        pair //= 2
    return scores


def _intra_diag_scores(bh, qh, kh, L):
    D = HG_D
    nb = L // SUBLANES
    b3 = bh.reshape(nb, SUBLANES, D)
    q3 = qh.reshape(nb, SUBLANES, D)
    k3 = kh.reshape(nb, SUBLANES, D)
    t_in = lax.broadcasted_iota(jnp.int32, (nb, SUBLANES, D), 1)
    lane = lax.broadcasted_iota(jnp.int32, (nb, SUBLANES, L), 2)
    blk0 = lax.broadcasted_iota(jnp.int32, (nb, SUBLANES, L), 0) * SUBLANES
    diag = jnp.zeros((nb, SUBLANES, L), F32)
    for s in range(SUBLANES):
        bs = b3[:, s:s + 1, :]
        ks = k3[:, s:s + 1, :]
        p = q3 * ks * jnp.exp2(jnp.where(t_in >= s, b3 - bs, NEG_BIG))
        col = jnp.sum(p, axis=-1, keepdims=True)
        diag = jnp.where(lane == blk0 + s, col, diag)
    return diag.reshape(L, L)


def _time_cumsum(lf, L):
    if L == SUBLANES:
        rows = [lf[0:1, :]]
        for r in range(1, L):
            rows.append(rows[-1] + lf[r:r + 1, :])
        return jnp.concatenate(rows, axis=0)
    tri = (lax.broadcasted_iota(jnp.int32, (L, L), 0) >= lax.broadcasted_iota(jnp.int32, (L, L), 1)).astype(BF16)
    l1 = lf.astype(BF16)
    r1 = lf - l1.astype(F32)
    l2 = r1.astype(BF16)
    l3 = (r1 - l2.astype(F32)).astype(BF16)
    return _dot(tri, l1) + _dot(tri, l2) + _dot(tri, l3)


def _mix_sequence_tile_steps(proj, consts, mixf, st_scr, ubuf, *, base, tt, chunk, t_real, unrolled):
    q_ref, k_ref, lf_ref, v_ref, gt_ref, zp_ref = proj
    cnt_ref, hgn_ref, pw_ref, ps_ref = consts
    D = HG_D
    L = chunk
    tr = tt if t_real is None else t_real
    n_chunks = pl.cdiv(tr, L)

    rows_t = pl.ds(base, tt)
    ubuf[16:16 + tt, :] = zp_ref[rows_t, :]
    for gi, w in enumerate(POOL_WINDOWS):
        ls = slice(gi * POOL_GC, (gi + 1) * POOL_GC)
        cur = ubuf[16:16 + tt, ls]
        acc = cur
        for j in range(1, w):
            acc = acc + ubuf[16 - j:16 - j + tt, ls]
        d = acc / cnt_ref[:, ls] - cur
        y = _dot(d.astype(BF16), pw_ref[gi]) * ps_ref[:, ls]
        mixf[rows_t, HG_WIDTH + gi * POOL_GC:HG_WIDTH + (gi + 1) * POOL_GC] = y
        yield

    def chunk_rows(c):
        r0 = base + c * L
        return pl.ds(r0 if isinstance(r0, int) else pl.multiple_of(r0, SUBLANES), L)

    def chunk_cumsum(c):
        lf = lf_ref[chunk_rows(c), :]
        if t_real is not None:
            lf = jnp.where((lax.broadcasted_iota(jnp.int32, lf.shape, 0) + c * L) < t_real, lf, 0.0)
        return _time_cumsum(lf, L)

    def chunk_steps(c, b, next_b):
        rows = chunk_rows(c)
        prepared = []
        for h in range(HG_HEADS):
            hs = slice(h * D, (h + 1) * D)
            bh = b[:, hs]
            qh = q_ref[rows, hs]
            kh = k_ref[rows, hs]
            if t_real is not None:
                kh = jnp.where((lax.broadcasted_iota(jnp.int32, (L, D), 0) + c * L) < t_real, kh, 0.0)
            prepared.append((hs, bh, qh, kh, _intra_block_operands(bh, qh, kh, L), (qh * jnp.exp2(bh)).astype(BF16)))
        yield
        heads = []
        for h, (hs, bh, qh, kh, operands, qe) in enumerate(prepared):
            products = [_dot_nt(qt, kt) for qt, kt in operands]
            st = st_scr[h]
            o_inter = _dot_nt(qe, st.astype(BF16))
            heads.append((hs, bh, kh, st, o_inter, products, _intra_diag_scores(bh, qh, kh, L)))
        if next_b is not None:
            next_b.append(chunk_cumsum(c + 1))
        yield
        results = []
        for hs, bh, kh, st, o_inter, products, diag in heads:
            vb = v_ref[rows, hs].astype(BF16)
            blast = bh[L - 1:L, :]
            scores = _sum_block_products(products, L) + diag
            o = o_inter + _dot(scores.astype(BF16), vb)
            kd = (kh * jnp.exp2(blast - bh)).astype(BF16)
            upd = lax.dot_general(vb, kd, (((0,), (0,)), ((), ())), preferred_element_type=F32)
            results.append((hs, st, blast, o, upd))
        yield
        for h, (hs, st, blast, o, upd) in enumerate(results):
            st_scr[h] = st * jnp.exp2(blast) + upd
            on = o * lax.rsqrt(jnp.mean(o * o, axis=-1, keepdims=True) + EPS) * hgn_ref[:, hs]
            mixf[rows, hs] = on * gt_ref[rows, hs]

    if unrolled or n_chunks == 1:
        b = chunk_cumsum(0)
        yield
        for c in range(n_chunks):
            next_b = [] if c + 1 < n_chunks else None
            yield from chunk_steps(c, b, next_b)
            b = next_b[0] if next_b else None
    else:
        lax.fori_loop(0, n_chunks, lambda c, _: (_run(chunk_steps(c, chunk_cumsum(c), None)), 0)[1], 0)
    if n_chunks * L < tt:
        mixf[pl.ds(base + n_chunks * L, tt - n_chunks * L), 0:HG_WIDTH] = jnp.zeros(
            (tt - n_chunks * L, HG_WIDTH), F32)


def _load_state(s0_ref, pp_ref, sidx, st_scr, ubuf):
    for h in range(HG_HEADS):
        st_scr[h] = s0_ref[sidx, h].T
    ubuf[1:1 + POOL_STATE, :] = pp_ref[sidx]


def _store_state(sout_ref, pout_ref, n, st_scr, ubuf, t_rows):
    pout_ref[n] = ubuf[t_rows + 1:t_rows + 1 + POOL_STATE, :]
    for h in range(HG_HEADS):
        sout_ref[n, h] = st_scr[h].T


def _mix_const_specs(cnt, layer):
    return [_fixed_spec(cnt.shape), _layer_spec((1, HG_WIDTH), layer),
            _layer_spec((len(POOL_WINDOWS), POOL_GC, POOL_GC), layer), _layer_spec((1, POOL_WIDTH), layer)]


def _pre_kernel(x_ref, g1_ref, w1i_ref, w1o_ref, gm_ref, win_ref, lbp_ref, x1_ref, *rest):
    _run(_pre_rows_steps(x_ref[...], g1_ref, w1i_ref, w1o_ref, gm_ref, win_ref, lbp_ref, rest[-1], x1_ref,
                         rest[:N_PROJ]))


def _pre_call(x_rows, weights, layer, tm):
    n_rows = x_rows.shape[0]
    row = lambda w: pl.BlockSpec((tm, w), lambda i: (i, 0))
    outs = [jax.ShapeDtypeStruct((n_rows, D_MODEL), F32)] + [jax.ShapeDtypeStruct((n_rows, HG_WIDTH), F32)] * N_PROJ
    return pl.pallas_call(
        _pre_kernel,
        grid=(n_rows // tm,),
        in_specs=[row(D_MODEL)] + _pre_weight_specs(weights, layer),
        out_specs=[row(D_MODEL)] + [row(HG_WIDTH)] * N_PROJ,
        out_shape=outs,
        scratch_shapes=[pltpu.VMEM((tm, D_MODEL), F32)],
        compiler_params=pltpu.CompilerParams(dimension_semantics=("arbitrary",),
                                             vmem_limit_bytes=VMEM_LIMIT),
        name="pre",
    )(x_rows, *weights)


def _premix_kernel(x_ref, g1_ref, w1i_ref, w1o_ref, gm_ref, win_ref, lbp_ref,
                   s0_ref, pp_ref, cnt_ref, hgn_ref, pw_ref, ps_ref, *rest, tiles_per_seq):
    n_alias = len(rest) - 4 - (N_PROJ + 4)
    x1_ref, mix_ref, sout_ref, pout_ref = rest[n_alias:n_alias + 4]
    proj_scr = rest[n_alias + 4:n_alias + 4 + N_PROJ]
    acc_ref, st_scr, ubuf, mixf = rest[n_alias + 4 + N_PROJ:]
    tm = ROW_TILE
    s = pl.program_id(0)
    last = pl.num_programs(0) - 1
    pos = lax.rem(jnp.maximum(s - 1, 0), tiles_per_seq)
    slot_mix = lax.rem(s + 1, 2)
    slot_pre = lax.rem(s, 2)

    @pl.when(jnp.logical_and(s > 0, pos == 0))
    def _():
        _load_state(s0_ref, pp_ref, 0, st_scr, ubuf)

    def mix_steps():
        return _mix_sequence_tile_steps([r.at[slot_mix] for r in proj_scr], (cnt_ref, hgn_ref, pw_ref, ps_ref),
                                        mixf, st_scr, ubuf, base=0, tt=tm, chunk=MIX_CHUNK, t_real=None,
                                        unrolled=True)

    def pre_steps():
        return _pre_rows_steps(x_ref[...], g1_ref, w1i_ref, w1o_ref, gm_ref, win_ref, lbp_ref, acc_ref, x1_ref,
                               [r.at[slot_pre] for r in proj_scr])

    @pl.when(s == 0)
    def _():
        _run(pre_steps())

    @pl.when(jnp.logical_and(s > 0, s < last))
    def _():
        n_mix_steps = len(POOL_WINDOWS) + 1 + 3 * (tm // MIX_CHUNK)
        _interleave(pre_steps(), N_FFN_STEPS, mix_steps(), n_mix_steps)

    @pl.when(s == last)
    def _():
        _run(mix_steps())

    @pl.when(s > 0)
    def _():
        mix_ref[...] = mixf[...].astype(BF16)
        ubuf[0:16, :] = ubuf[tm:tm + 16, :]

    @pl.when(jnp.logical_and(s > 0, pos == tiles_per_seq - 1))
    def _():
        _store_state(sout_ref, pout_ref, 0, st_scr, ubuf, tm)


def _premix_call(x_rows, weights, s0, pp, cnt, mix_consts, prev, *, layer, seq_len):
    m = x_rows.shape[0]
    tm = ROW_TILE
    n = m // tm
    tiles_per_seq = seq_len // tm
    n_seq = m // seq_len
    mix_tile = lambda s: jnp.maximum(s - 1, 0)
    in_specs = ([pl.BlockSpec((tm, D_MODEL), lambda s: (jnp.minimum(s, n - 1), 0))]
                + _pre_weight_specs(weights, layer)
                + [pl.BlockSpec((None, 1) + SDIMS, lambda s: (layer, 0, 0, 0, 0)),
                   pl.BlockSpec((None, 1) + PDIMS, lambda s: (layer, 0, 0, 0))]
                + _mix_const_specs(cnt, layer))
    args = [x_rows, *weights, s0, pp, cnt, *mix_consts]
    aliases = {}
    for out_idx, buf in zip((2, 3), prev or ()):
        in_specs.append(pl.BlockSpec(memory_space=pl.ANY))
        aliases[len(args)] = out_idx
        args.append(buf)
    return pl.pallas_call(
        functools.partial(_premix_kernel, tiles_per_seq=tiles_per_seq),
        grid=(n + 1,),
        in_specs=in_specs,
        input_output_aliases=aliases,
        out_specs=[pl.BlockSpec((tm, D_MODEL), lambda s: (jnp.minimum(s, n - 1), 0)),
                   pl.BlockSpec((tm, D_MODEL), lambda s: (mix_tile(s), 0)),
                   pl.BlockSpec((None, 1) + SDIMS, lambda s: (layer, mix_tile(s) // tiles_per_seq, 0, 0, 0)),
                   pl.BlockSpec((None, 1) + PDIMS, lambda s: (layer, mix_tile(s) // tiles_per_seq, 0, 0))],
        out_shape=[jax.ShapeDtypeStruct((m, D_MODEL), F32), jax.ShapeDtypeStruct((m, D_MODEL), BF16),
                   jax.ShapeDtypeStruct((DEPTH, n_seq) + SDIMS, F32),
                   jax.ShapeDtypeStruct((DEPTH, n_seq) + PDIMS, F32)],
        scratch_shapes=[pltpu.VMEM((2, tm, HG_WIDTH), F32)] * N_PROJ
        + [pltpu.VMEM((tm, D_MODEL), F32), pltpu.VMEM(SDIMS, F32),
           pltpu.VMEM((16 + tm, POOL_WIDTH), F32), pltpu.VMEM((tm, D_MODEL), F32)],
        compiler_params=pltpu.CompilerParams(dimension_semantics=("arbitrary",),
                                             vmem_limit_bytes=VMEM_LIMIT),
        name="premix",
    )(*args)


def _post_kernel(x1_ref, mix_ref, wo_ref, g2_ref, w2i_ref, w2o_ref, gf_ref, *rest, final, n_cast):
    cast_in = rest[:n_cast]
    y_ref = rest[n_cast]
    cast_out = rest[n_cast + 1:2 * n_cast + 1]
    acc_ref = rest[-1]
    x2 = x1_ref[...] + _dot(mix_ref[...].astype(BF16), wo_ref[...])
    _run(_ffn_half_steps(x2, g2_ref[...], w2i_ref, w2o_ref, acc_ref))
    x3 = x2 + 0.5 * acc_ref[...]
    y_ref[...] = _rmsnorm(x3, gf_ref[...]) if final else x3
    for src, dst in zip(cast_in, cast_out):
        dst[...] = src[...].astype(BF16)


def _cast_block_rows(rows, n_steps):
    bf16_rows = 2 * SUBLANES
    per_step = rows // n_steps
    hold = 1
    while per_step * hold % bf16_rows:
        hold *= 2
    assert rows % (per_step * hold) == 0 and n_steps % hold == 0
    return per_step * hold, hold


def _post_call(x1_rows, mix_rows, weights, gf, layer, final, tm, cast=None):
    m = x1_rows.shape[0]
    n_steps = m // tm
    row = lambda w: pl.BlockSpec((tm, w), lambda i: (i, 0))
    cast_arrays, cast_layer = cast or ((), None)
    cast_in_specs, cast_out_specs, cast_shapes = [], [], []
    for a in cast_arrays:
        _, rows, cols = a.shape
        blk, hold = _cast_block_rows(rows, n_steps)
        cast_in_specs.append(pl.BlockSpec((None, blk, cols), lambda i, hold=hold: (cast_layer, i // hold, 0)))
        cast_out_specs.append(pl.BlockSpec((blk, cols), lambda i, hold=hold: (i // hold, 0)))
        cast_shapes.append(jax.ShapeDtypeStruct((rows, cols), BF16))
    out = pl.pallas_call(
        functools.partial(_post_kernel, final=final, n_cast=len(cast_arrays)),
        grid=(n_steps,),
        in_specs=[row(D_MODEL), row(D_MODEL)] + [_param_spec(w, layer) for w in weights]
        + [_fixed_spec((1, D_MODEL))] + cast_in_specs,
        out_specs=[row(D_MODEL)] + cast_out_specs,
        out_shape=[jax.ShapeDtypeStruct((m, D_MODEL), F32)] + cast_shapes,
        scratch_shapes=[pltpu.VMEM((tm, D_MODEL), F32)],
        compiler_params=pltpu.CompilerParams(dimension_semantics=("arbitrary",),
                                             vmem_limit_bytes=VMEM_LIMIT),
        name="post",
    )(x1_rows, mix_rows, *weights, gf, *cast_arrays)
    return out[0], tuple(out[1:])


def _split3_rows(e):
    e1 = e.astype(BF16).astype(F32)
    r1 = e - e1
    e2 = r1.astype(BF16).astype(F32)
    e3 = r1 - e2
    row = lax.broadcasted_iota(jnp.int32, (SUBLANES, e.shape[1]), 0)
    return jnp.where(row == 0, e1, jnp.where(row == 1, e2, jnp.where(row == 2, e3, 0.0))).astype(BF16)


def _dot_tn(a, b):
    return lax.dot_general(a, b, (((0,), (0,)), ((), ())), preferred_element_type=F32)


def _sample_mix_kernel(q_ref, k_ref, lf_ref, v_ref, gt_ref, zp_ref, s0_ref, pp_ref, cnt_ref, hgn_ref, pw_ref,
                       ps_ref, *rest, nseq, group, t_real):
    mix_ref, sout_ref, pout_ref, ubuf, dbuf, mixf = rest[-6:]
    D = HG_D
    T = SUBLANES
    row_w = lax.broadcasted_iota(jnp.int32, (T, HG_WIDTH), 0)
    row_d = lax.broadcasted_iota(jnp.int32, (T, D), 0)
    ones = jnp.ones((T, D), BF16)

    def group_step(g):
        prepared = []
        for j in range(group):
            n = g * group + j
            rows = pl.ds(pl.multiple_of(n * T, T), T)
            lf = jnp.where(row_w < t_real, lf_ref[rows, :], 0.0)
            b = lf
            for shift in (1, 2, 4):
                b = b + jnp.where(row_w >= shift, pltpu.roll(b, shift, 0), 0.0)
            for h in range(HG_HEADS):
                hs = slice(h * D, (h + 1) * D)
                bh = b[:, hs]
                qh = q_ref[rows, hs]
                kh = jnp.where(row_d < t_real, k_ref[rows, hs], 0.0)
                vh = v_ref[rows, hs]
                blast = bh[T - 1:T, :]
                qe = (qh * jnp.exp2(bh)).astype(BF16)
                kd = (kh * jnp.exp2(blast - bh)).astype(BF16)
                dec = _split3_rows(jnp.exp2(blast))
                o_intra = jnp.zeros((T, D), F32)
                for s in range(t_real):
                    p = qh * kh[s:s + 1, :] * jnp.exp2(jnp.where(row_d >= s, bh - bh[s:s + 1, :], NEG_BIG))
                    o_intra = o_intra + jnp.sum(p, axis=-1, keepdims=True) * vh[s:s + 1, :]
                prepared.append((n, rows, h, hs, qe, kd, vh.astype(BF16), dec, o_intra))
        products = []
        for n, rows, h, hs, qe, kd, vb, dec, o_intra in prepared:
            st = s0_ref[n, h]
            products.append((st, _dot(qe, st.astype(BF16)), _dot_tn(kd, vb), _dot_tn(dec, ones)))
        for (n, rows, h, hs, qe, kd, vb, dec, o_intra), (st, o_inter, upd, dec_rows) in zip(prepared, products):
            sout_ref[n, h] = st * dec_rows + upd
            o = o_inter + o_intra
            on = o * lax.rsqrt(jnp.mean(o * o, axis=-1, keepdims=True) + EPS) * hgn_ref[:, hs]
            mixf[rows, hs] = on * gt_ref[rows, hs]
        for j in range(group):
            n = g * group + j
            rows = pl.ds(pl.multiple_of(n * T, T), T)
            ubuf[j, 1:1 + POOL_STATE, :] = pp_ref[n]
            ubuf[j, 16:16 + T, :] = zp_ref[rows, :]
            for gi, w in enumerate(POOL_WINDOWS):
                ls = slice(gi * POOL_GC, (gi + 1) * POOL_GC)
                cur = ubuf[j, 16:16 + T, ls]
                acc = cur
                for jj in range(1, w):
                    acc = acc + ubuf[j, 16 - jj:16 - jj + T, ls]
                dbuf[rows, ls] = acc / cnt_ref[:, ls] - cur
            pout_ref[n] = ubuf[j, t_real + 1:t_real + 1 + POOL_STATE, :]

    lax.fori_loop(0, nseq // group, lambda g, _: (group_step(g), 0)[1], 0)
    for gi in range(len(POOL_WINDOWS)):
        ls = slice(gi * POOL_GC, (gi + 1) * POOL_GC)
        mixf[:, HG_WIDTH + gi * POOL_GC:HG_WIDTH + (gi + 1) * POOL_GC] = (
            _dot(dbuf[:, ls].astype(BF16), pw_ref[gi]) * ps_ref[:, ls])
    mix_ref[...] = mixf[...].astype(mix_ref.dtype)


def _sample_mix_call(proj, s0, pp, cnt, mix_consts, prev, *, layer, row_offset, n_seq_total, nseq, group, t_real,
                     mix_dtype):
    m = proj[0].shape[0]
    assert n_seq_total % nseq == 0 and nseq % group == 0
    r = nseq * SUBLANES
    assert row_offset % r == 0
    blk0 = row_offset // r
    rows = lambda w: pl.BlockSpec((r, w), lambda bi: (blk0 + bi, 0))
    in_specs = ([rows(HG_WIDTH)] * N_PROJ
                + [pl.BlockSpec((None, nseq) + SDIMS, lambda bi: (layer, bi, 0, 0, 0)),
                   pl.BlockSpec((None, nseq) + PDIMS, lambda bi: (layer, bi, 0, 0))]
                + _mix_const_specs(cnt, layer))
    args = [*proj, s0, pp, cnt, *mix_consts]
    aliases = {}
    for out_idx, buf in enumerate(prev):
        if buf is not None:
            in_specs.append(pl.BlockSpec(memory_space=pl.ANY))
            aliases[len(args)] = out_idx
            args.append(buf)
    return pl.pallas_call(
        functools.partial(_sample_mix_kernel, nseq=nseq, group=group, t_real=t_real),
        grid=(n_seq_total // nseq,),
        in_specs=in_specs,
        input_output_aliases=aliases,
        out_specs=[pl.BlockSpec((r, D_MODEL), lambda bi: (blk0 + bi, 0)),
                   pl.BlockSpec((None, nseq) + SDIMS, lambda bi: (layer, bi, 0, 0, 0)),
                   pl.BlockSpec((None, nseq) + PDIMS, lambda bi: (layer, bi, 0, 0))],
        out_shape=[jax.ShapeDtypeStruct((m, D_MODEL), mix_dtype),
                   jax.ShapeDtypeStruct((DEPTH, n_seq_total) + SDIMS, F32),
                   jax.ShapeDtypeStruct((DEPTH, n_seq_total) + PDIMS, F32)],
        scratch_shapes=[pltpu.VMEM((group, 16 + SUBLANES, POOL_WIDTH), F32), pltpu.VMEM((r, POOL_WIDTH), F32),
                        pltpu.VMEM((r, D_MODEL), F32)],
        compiler_params=pltpu.CompilerParams(dimension_semantics=("arbitrary",),
                                             vmem_limit_bytes=VMEM_LIMIT),
        name="mix_sample",
    )(*args)


def _mix_kernel(q_ref, k_ref, lf_ref, v_ref, gt_ref, zp_ref, s0_ref, pp_ref, cnt_ref, hgn_ref, pw_ref, ps_ref,
                *rest, tt, chunk, nseq, t_real):
    mix_ref, sout_ref, pout_ref, st_scr, ubuf, mixf = rest[-6:]
    proj = (q_ref, k_ref, lf_ref, v_ref, gt_ref, zp_ref)
    consts = (cnt_ref, hgn_ref, pw_ref, ps_ref)

    def seq_step(n):
        _load_state(s0_ref, pp_ref, n, st_scr, ubuf)
        base = n * tt if isinstance(n, int) else pl.multiple_of(n * tt, SUBLANES)
        _run(_mix_sequence_tile_steps(proj, consts, mixf, st_scr, ubuf, base=base, tt=tt, chunk=chunk,
                                      t_real=t_real, unrolled=False))
        _store_state(sout_ref, pout_ref, n, st_scr, ubuf, t_real)

    if nseq == 1:
        seq_step(0)
    else:
        lax.fori_loop(0, nseq, lambda n, _: (seq_step(n), 0)[1], 0)
    mix_ref[...] = mixf[...].astype(mix_ref.dtype)


def _mix_call(proj, s0, s0_layer, pp, pp_layer, cnt, mix_consts, prev, *, layer, row_offset, n_seq_total,
              tt, chunk, nseq, t_real, mix_dtype):
    m = proj[0].shape[0]
    assert n_seq_total % nseq == 0
    r = nseq * tt
    assert row_offset % r == 0
    blk0 = row_offset // r
    rows = lambda w: pl.BlockSpec((r, w), lambda bi: (blk0 + bi, 0))
    in_specs = ([rows(HG_WIDTH)] * N_PROJ
                + [pl.BlockSpec((None, nseq) + SDIMS, lambda bi: (s0_layer, bi, 0, 0, 0)),
                   pl.BlockSpec((None, nseq) + PDIMS, lambda bi: (pp_layer, bi, 0, 0))]
                + _mix_const_specs(cnt, layer))
    args = [*proj, s0, pp, cnt, *mix_consts]
    aliases = {}
    for out_idx, buf in enumerate(prev):
        if buf is not None:
            in_specs.append(pl.BlockSpec(memory_space=pl.ANY))
            aliases[len(args)] = out_idx
            args.append(buf)
    kern = functools.partial(_mix_kernel, tt=tt, chunk=chunk, nseq=nseq, t_real=t_real)
    return pl.pallas_call(
        kern,
        grid=(n_seq_total // nseq,),
        in_specs=in_specs,
        input_output_aliases=aliases,
        out_specs=[pl.BlockSpec((r, D_MODEL), lambda bi: (blk0 + bi, 0)),
                   pl.BlockSpec((None, nseq) + SDIMS, lambda bi: (layer, bi, 0, 0, 0)),
                   pl.BlockSpec((None, nseq) + PDIMS, lambda bi: (layer, bi, 0, 0))],
        out_shape=[jax.ShapeDtypeStruct((m, D_MODEL), mix_dtype),
                   jax.ShapeDtypeStruct((DEPTH, n_seq_total) + SDIMS, F32),
                   jax.ShapeDtypeStruct((DEPTH, n_seq_total) + PDIMS, F32)],
        scratch_shapes=[pltpu.VMEM(SDIMS, F32), pltpu.VMEM((16 + tt, POOL_WIDTH), F32),
                        pltpu.VMEM((r, D_MODEL), F32)],
        compiler_params=pltpu.CompilerParams(dimension_semantics=("arbitrary",),
                                             vmem_limit_bytes=VMEM_LIMIT),
        name="mix",
    )(*args)


def _lower_bound_params(lb_logits):
    p = jax.nn.softmax(lb_logits.astype(F32), axis=0)
    cs = jnp.cumsum(p, axis=0)
    lb = cs - cs[0:1]
    rows = jnp.stack([1.0 - lb, jnp.log(lb), jnp.log1p(-lb)], axis=1)
    return jnp.pad(rows, ((0, 0), (0, SUBLANES - 3), (0, 0)))


def kernel(x_prompt, x_sample, state_hgrn, state_pool, meta, lb_logits, norm_ffn1, w_ffn1_in, w_ffn1_out,
           norm_mix, w_in, hg_norm, pool_w, pool_scale, w_out, norm_ffn2, w_ffn2_in, w_ffn2_out, norm_final):
    bp, sp, _ = x_prompt.shape
    bs, ss, _ = x_sample.shape
    ss_pad = SUBLANES
    n_prompt = bp * sp
    n_sample = bs * ss_pad
    xp = x_prompt.reshape(n_prompt, D_MODEL)
    x_side = jnp.concatenate([jnp.pad(x_sample, ((0, 0), (0, ss_pad - ss), (0, 0))).reshape(n_sample, D_MODEL),
                              meta.astype(F32)], axis=0)

    gain3 = lambda a: a.reshape(DEPTH, 1, -1)
    big_f32 = (w_ffn1_in, w_ffn1_out, w_in, w_out, w_ffn2_in, w_ffn2_out)
    big = tuple(w[0].astype(BF16) for w in big_f32)
    lbp = _lower_bound_params(lb_logits)
    mix_consts = (hg_norm.reshape(DEPTH, 1, HG_WIDTH), pool_w.astype(BF16), gain3(pool_scale))
    gf = norm_final.reshape(1, D_MODEL)
    wmax = jnp.repeat(jnp.asarray(POOL_WINDOWS, F32), POOL_GC)[None, :]
    cnt_meta = jnp.minimum(wmax, jnp.arange(1, N_META + 1, dtype=F32)[:, None])
    zero_s = jnp.zeros((1, 1) + SDIMS, F32)
    zero_p = jnp.zeros((1, 1) + PDIMS, F32)

    st_m = st_p = st_s = (None, None)
    for l in range(DEPTH):
        last = l == DEPTH - 1
        w1i, w1o, wi, wo, w2i, w2o = big
        pre_w = (gain3(norm_ffn1), w1i, w1o, gain3(norm_mix), wi, lbp)
        post_w = (wo, gain3(norm_ffn2), w2i, w2o)
        x1_side, *proj = _pre_call(x_side, pre_w, l, SIDE_TILE)
        mix_side, *st_m = _mix_call(proj, zero_s, 0, zero_p, 0, cnt_meta, mix_consts, (None, *st_m),
                                    layer=l, row_offset=n_sample, n_seq_total=1,
                                    tt=N_META, chunk=SUBLANES, nseq=1, t_real=N_META, mix_dtype=F32)
        x1, mix, *st_p = _premix_call(xp, pre_w, st_m[0], st_m[1], wmax, mix_consts,
                                      None if l == 0 else st_p, layer=l, seq_len=sp)
        mix_side, *st_s = _sample_mix_call(proj, state_hgrn, state_pool, wmax, mix_consts, (mix_side, *st_s),
                                           layer=l, row_offset=0, n_seq_total=bs, nseq=SAMPLE_SEQS,
                                           group=SAMPLE_GROUP, t_real=ss, mix_dtype=F32)
        xp, big_next = _post_call(x1, mix, post_w, gf, l, last, ROW_TILE, None if last else (big_f32, l + 1))
        x_side, _ = _post_call(x1_side, mix_side, post_w, gf, l, last, SIDE_TILE)
        big = big_next

    y_prompt = xp.reshape(bp, sp, D_MODEL)
    y_sample = x_side[:n_sample].reshape(bs, ss_pad, D_MODEL)[:, :ss]
    return (y_prompt, y_sample, st_p[0], st_p[1], st_s[0], st_s[1])
```

```python
import functools

import jax
import jax.numpy as jnp
from jax import lax
from jax.experimental import pallas as pl
from jax.experimental.pallas import tpu as pltpu

F32 = jnp.float32
BF16 = jnp.bfloat16

D_MODEL = 1024
DEPTH = 4
N_META = 16
HG_WIDTH = 512
HG_HEADS = 4
HG_D = 128
POOL_WIDTH = 512
POOL_WINDOWS = (2, 4, 8, 16)
POOL_GC = 128
POOL_STATE = 15
IN_COLS = 4 * HG_WIDTH + POOL_WIDTH
D_FF = 2816
EPS = 1e-6
N_PROJ = 6

SUBLANES = 8
MXU_DIM = 256
FF_CHUNK = MXU_DIM
ROW_TILE = 512
MIX_CHUNK = 64
SAMPLE_SEQS = 16
SAMPLE_GROUP = 4
VMEM_LIMIT = 56 * 1024 * 1024
NEG_BIG = -1e30
LOG2E = 1.4426950408889634

SDIMS = (HG_HEADS, HG_D, HG_D)
PDIMS = (POOL_STATE, POOL_WIDTH)


def _layer_spec(shape, layer):
    nd = len(shape)
    return pl.BlockSpec((None,) + tuple(shape), lambda *_: (layer,) + (0,) * nd, pipeline_mode=pl.Buffered(1))


def _fixed_spec(shape):
    nd = len(shape)
    return pl.BlockSpec(tuple(shape), lambda *_: (0,) * nd)


def _param_spec(arr, layer):
    if arr.ndim == 3:
        return _layer_spec(arr.shape[1:], layer)
    return pl.BlockSpec(arr.shape, lambda *_: (0, 0), pipeline_mode=pl.Buffered(1))


def _rmsnorm(x, gain):
    return x * lax.rsqrt(jnp.mean(x * x, axis=-1, keepdims=True) + EPS) * gain


def _dot(a, b):
    return jnp.dot(a, b, preferred_element_type=F32)


def _exp_neg(z):
    return jnp.exp2(z * (-LOG2E))


def _silu(z):
    return z * (1.0 / (1.0 + _exp_neg(z)))


def _run(steps):
    for _ in steps:
        pass


def _interleave(steps_a, n_a, steps_b, n_b):
    done_b = 0
    for i, _ in enumerate(steps_a):
        while done_b < min(n_b, ((i + 1) * n_b) // n_a):
            next(steps_b)
            done_b += 1
    _run(steps_b)


def _ffn_half_steps(x, gain, win_ref, wout_ref, acc_ref):
    hn = _rmsnorm(x, gain).astype(BF16)
    for c in range(D_FF // FF_CHUNK):
        lo = c * FF_CHUNK
        gate = _dot(hn, win_ref[:, lo:lo + FF_CHUNK])
        up = _dot(hn, win_ref[:, D_FF + lo:D_FF + lo + FF_CHUNK])
        yield
        part = _dot((_silu(gate) * up).astype(BF16), wout_ref[lo:lo + FF_CHUNK, :])
        if c == 0:
            acc_ref[...] = part
        else:
            acc_ref[...] += part
        yield


N_FFN_STEPS = 2 * (D_FF // FF_CHUNK)
N_PRE_STEPS = N_FFN_STEPS + 5


def _pre_rows_steps(x, g1_ref, w1i_ref, w1o_ref, gm_ref, win_ref, lbp_ref, acc_ref, x1_ref, proj_refs):
    q_ref, k_ref, lf_ref, v_ref, gt_ref, zp_ref = proj_refs
    yield from _ffn_half_steps(x, g1_ref[...], w1i_ref, w1o_ref, acc_ref)
    x1 = x + 0.5 * acc_ref[...]
    x1_ref[...] = x1
    hn = _rmsnorm(x1, gm_ref[...]).astype(BF16)
    W = HG_WIDTH
    zf = _dot(hn, win_ref[:, W:2 * W])
    yield
    q_ref[...] = _silu(_dot(hn, win_ref[:, 0:W]))
    yield
    v_ref[...] = _dot(hn, win_ref[:, 2 * W:3 * W])
    yield
    gt_ref[...] = _silu(_dot(hn, win_ref[:, 3 * W:4 * W]))
    yield
    zp_ref[...] = _dot(hn, win_ref[:, 4 * W:5 * W])
    yield
    one_m_lb = lbp_ref[0:1, :]
    log_lb = lbp_ref[1:2, :]
    log1m_lb = lbp_ref[2:3, :]
    e = _exp_neg(jnp.abs(zf))
    k_ref[...] = one_m_lb * (jnp.where(zf >= 0.0, e, 1.0) * (1.0 / (1.0 + e)))
    c = log1m_lb + (jnp.minimum(zf, 0.0) - jnp.log(1.0 + e))
    lf_ref[...] = (jnp.maximum(log_lb, c) + jnp.log(1.0 + _exp_neg(jnp.abs(log_lb - c)))) * LOG2E


def _pre_weight_specs(weights, layer):
    return [_param_spec(w, layer) for w in weights]


def _intra_block_operands(bh, qh, kh, L):
    D = HG_D
    row_l = lax.broadcasted_iota(jnp.int32, (L, D), 0)
    operands = []
    half = L // 2
    while half >= SUBLANES:
        pair = 2 * half
        pieces = [jnp.broadcast_to(bh[p * pair + half - 1:p * pair + half, :], (pair, D))
                  for p in range(L // pair)]
        bref = pieces[0] if len(pieces) == 1 else jnp.concatenate(pieces, axis=0)
        second = (row_l & half) != 0
        qt = (qh * jnp.exp2(jnp.where(second, bh - bref, NEG_BIG))).astype(BF16)
        kt = (kh * jnp.exp2(jnp.where(second, NEG_BIG, bref - bh))).astype(BF16)
        operands.append((qt, kt))
        half //= 2
    return operands


def _dot_nt(a, b):
    return lax.dot_general(a, b, (((1,), (1,)), ((), ())), preferred_element_type=F32)


def _sum_block_products(products, L):
    row_i = lax.broadcasted_iota(jnp.int32, (L, L), 0)
    col_i = lax.broadcasted_iota(jnp.int32, (L, L), 1)
    scores = jnp.zeros((L, L), F32)
    pair = L
    for blk in products:
        scores = scores + jnp.where((row_i & -pair) == (col_i & -pair), blk, 0.0)
        pair //= 2
    return scores


def _intra_diag_scores(bh, qh, kh, L):
    D = HG_D
    nb = L // SUBLANES
    b3 = bh.reshape(nb, SUBLANES, D)
    q3 = qh.reshape(nb, SUBLANES, D)
    k3 = kh.reshape(nb, SUBLANES, D)
    t_in = lax.broadcasted_iota(jnp.int32, (nb, SUBLANES, D), 1)
    lane = lax.broadcasted_iota(jnp.int32, (nb, SUBLANES, L), 2)
    blk0 = lax.broadcasted_iota(jnp.int32, (nb, SUBLANES, L), 0) * SUBLANES
    diag = jnp.zeros((nb, SUBLANES, L), F32)
    for s in range(SUBLANES):
        bs = b3[:, s:s + 1, :]
        ks = k3[:, s:s + 1, :]
        p = q3 * ks * jnp.exp2(jnp.where(t_in >= s, b3 - bs, NEG_BIG))
        col = jnp.sum(p, axis=-1, keepdims=True)
        diag = jnp.where(lane == blk0 + s, col, diag)
    return diag.reshape(L, L)


def _time_cumsum(lf, L):
    if L == SUBLANES:
        rows = [lf[0:1, :]]
        for r in range(1, L):
            rows.append(rows[-1] + lf[r:r + 1, :])
        return jnp.concatenate(rows, axis=0)
    tri = (lax.broadcasted_iota(jnp.int32, (L, L), 0) >= lax.broadcasted_iota(jnp.int32, (L, L), 1)).astype(BF16)
    l1 = lf.astype(BF16)
    r1 = lf - l1.astype(F32)
    l2 = r1.astype(BF16)
    l3 = (r1 - l2.astype(F32)).astype(BF16)
    return _dot(tri, l1) + _dot(tri, l2) + _dot(tri, l3)


def _mix_sequence_tile_steps(proj, consts, mixf, st_scr, ubuf, *, base, tt, chunk, t_real, unrolled):
    q_ref, k_ref, lf_ref, v_ref, gt_ref, zp_ref = proj
    cnt_ref, hgn_ref, pw_ref, ps_ref = consts
    D = HG_D
    L = chunk
    tr = tt if t_real is None else t_real
    n_chunks = pl.cdiv(tr, L)

    rows_t = pl.ds(base, tt)
    ubuf[16:16 + tt, :] = zp_ref[rows_t, :]
    for gi, w in enumerate(POOL_WINDOWS):
        ls = slice(gi * POOL_GC, (gi + 1) * POOL_GC)
        cur = ubuf[16:16 + tt, ls]
        acc = cur
        for j in range(1, w):
            acc = acc + ubuf[16 - j:16 - j + tt, ls]
        d = acc / cnt_ref[:, ls] - cur
        y = _dot(d.astype(BF16), pw_ref[gi]) * ps_ref[:, ls]
        mixf[rows_t, HG_WIDTH + gi * POOL_GC:HG_WIDTH + (gi + 1) * POOL_GC] = y
        yield

    def chunk_rows(c):
        r0 = base + c * L
        return pl.ds(r0 if isinstance(r0, int) else pl.multiple_of(r0, SUBLANES), L)

    def chunk_cumsum(c):
        lf = lf_ref[chunk_rows(c), :]
        if t_real is not None:
            lf = jnp.where((lax.broadcasted_iota(jnp.int32, lf.shape, 0) + c * L) < t_real, lf, 0.0)
        return _time_cumsum(lf, L)

    def chunk_steps(c, b, next_b):
        rows = chunk_rows(c)
        prepared = []
        for h in range(HG_HEADS):
            hs = slice(h * D, (h + 1) * D)
            bh = b[:, hs]
            qh = q_ref[rows, hs]
            kh = k_ref[rows, hs]
            if t_real is not None:
                kh = jnp.where((lax.broadcasted_iota(jnp.int32, (L, D), 0) + c * L) < t_real, kh, 0.0)
            prepared.append((hs, bh, qh, kh, _intra_block_operands(bh, qh, kh, L), (qh * jnp.exp2(bh)).astype(BF16)))
        yield
        heads = []
        for h, (hs, bh, qh, kh, operands, qe) in enumerate(prepared):
            products = [_dot_nt(qt, kt) for qt, kt in operands]
            st = st_scr[h]
            o_inter = _dot_nt(qe, st.astype(BF16))
            heads.append((hs, bh, kh, st, o_inter, products, _intra_diag_scores(bh, qh, kh, L)))
        if next_b is not None:
            next_b.append(chunk_cumsum(c + 1))
        yield
        results = []
        for hs, bh, kh, st, o_inter, products, diag in heads:
            vb = v_ref[rows, hs].astype(BF16)
            blast = bh[L - 1:L, :]
            scores = _sum_block_products(products, L) + diag
            o = o_inter + _dot(scores.astype(BF16), vb)
            kd = (kh * jnp.exp2(blast - bh)).astype(BF16)
            upd = lax.dot_general(vb, kd, (((0,), (0,)), ((), ())), preferred_element_type=F32)
            results.append((hs, st, blast, o, upd))
        yield
        for h, (hs, st, blast, o, upd) in enumerate(results):
            st_scr[h] = st * jnp.exp2(blast) + upd
            on = o * lax.rsqrt(jnp.mean(o * o, axis=-1, keepdims=True) + EPS) * hgn_ref[:, hs]
            mixf[rows, hs] = on * gt_ref[rows, hs]

    if unrolled or n_chunks == 1:
        b = chunk_cumsum(0)
        yield
        for c in range(n_chunks):
            next_b = [] if c + 1 < n_chunks else None
            yield from chunk_steps(c, b, next_b)
            b = next_b[0] if next_b else None
    else:
        lax.fori_loop(0, n_chunks, lambda c, _: (_run(chunk_steps(c, chunk_cumsum(c), None)), 0)[1], 0)
    if n_chunks * L < tt:
        mixf[pl.ds(base + n_chunks * L, tt - n_chunks * L), 0:HG_WIDTH] = jnp.zeros(
            (tt - n_chunks * L, HG_WIDTH), F32)


def _load_state(s0_ref, pp_ref, sidx, st_scr, ubuf):
    for h in range(HG_HEADS):
        st_scr[h] = s0_ref[sidx, h].T
    ubuf[1:1 + POOL_STATE, :] = pp_ref[sidx]


def _store_state(sout_ref, pout_ref, n, st_scr, ubuf, t_rows):
    pout_ref[n] = ubuf[t_rows + 1:t_rows + 1 + POOL_STATE, :]
    for h in range(HG_HEADS):
        sout_ref[n, h] = st_scr[h].T


def _mix_const_specs(cnt, layer):
    return [_fixed_spec(cnt.shape), _layer_spec((1, HG_WIDTH), layer),
            _layer_spec((len(POOL_WINDOWS), POOL_GC, POOL_GC), layer), _layer_spec((1, POOL_WIDTH), layer)]


def _pre_kernel(x_ref, g1_ref, w1i_ref, w1o_ref, gm_ref, win_ref, lbp_ref, x1_ref, *rest):
    _run(_pre_rows_steps(x_ref[...], g1_ref, w1i_ref, w1o_ref, gm_ref, win_ref, lbp_ref, rest[-1], x1_ref,
                         rest[:N_PROJ]))


def _pre_call(x_rows, weights, layer, tm):
    n_rows = x_rows.shape[0]
    row = lambda w: pl.BlockSpec((tm, w), lambda i: (i, 0))
    outs = [jax.ShapeDtypeStruct((n_rows, D_MODEL), F32)] + [jax.ShapeDtypeStruct((n_rows, HG_WIDTH), F32)] * N_PROJ
    return pl.pallas_call(
        _pre_kernel,
        grid=(n_rows // tm,),
        in_specs=[row(D_MODEL)] + _pre_weight_specs(weights, layer),
        out_specs=[row(D_MODEL)] + [row(HG_WIDTH)] * N_PROJ,
        out_shape=outs,
        scratch_shapes=[pltpu.VMEM((tm, D_MODEL), F32)],
        compiler_params=pltpu.CompilerParams(dimension_semantics=("arbitrary",),
                                             vmem_limit_bytes=VMEM_LIMIT),
        name="pre",
    )(x_rows, *weights)


def _premix_kernel(x_ref, g1_ref, w1i_ref, w1o_ref, gm_ref, win_ref, lbp_ref,
                   s0_ref, pp_ref, cnt_ref, hgn_ref, pw_ref, ps_ref, *rest, tiles_per_seq):
    n_alias = len(rest) - 4 - (N_PROJ + 4)
    x1_ref, mix_ref, sout_ref, pout_ref = rest[n_alias:n_alias + 4]
    proj_scr = rest[n_alias + 4:n_alias + 4 + N_PROJ]
    acc_ref, st_scr, ubuf, mixf = rest[n_alias + 4 + N_PROJ:]
    tm = ROW_TILE
    s = pl.program_id(0)
    t = jnp.maximum(s - 1, 0)
    pos = lax.rem(t, tiles_per_seq)

    @pl.when(s == 0)
    def _():
        for r in proj_scr:
            r[...] = jnp.zeros(r.shape, F32)

    @pl.when(pos == 0)
    def _():
        _load_state(s0_ref, pp_ref, 0, st_scr, ubuf)

    slot_mix = lax.rem(s + 1, 2)
    slot_pre = lax.rem(s, 2)
    mix_steps = _mix_sequence_tile_steps([r.at[slot_mix] for r in proj_scr], (cnt_ref, hgn_ref, pw_ref, ps_ref),
                                         mixf, st_scr, ubuf, base=0, tt=tm, chunk=MIX_CHUNK, t_real=None,
                                         unrolled=True)
    pre_steps = _pre_rows_steps(x_ref[...], g1_ref, w1i_ref, w1o_ref, gm_ref, win_ref, lbp_ref, acc_ref, x1_ref,
                                [r.at[slot_pre] for r in proj_scr])
    n_mix_steps = len(POOL_WINDOWS) + 1 + 3 * (tm // MIX_CHUNK)
    _interleave(pre_steps, N_FFN_STEPS, mix_steps, n_mix_steps)
    mix_ref[...] = mixf[...].astype(BF16)

    @pl.when(jnp.logical_and(s > 0, pos == tiles_per_seq - 1))
    def _():
        _store_state(sout_ref, pout_ref, 0, st_scr, ubuf, tm)

    ubuf[0:16, :] = ubuf[tm:tm + 16, :]


def _premix_call(x_rows, weights, s0, pp, cnt, mix_consts, prev, *, layer, seq_len):
    m = x_rows.shape[0]
    tm = ROW_TILE
    n = m // tm
    tiles_per_seq = seq_len // tm
    n_seq = m // seq_len
    mix_tile = lambda s: jnp.maximum(s - 1, 0)
    in_specs = ([pl.BlockSpec((tm, D_MODEL), lambda s: (jnp.minimum(s, n - 1), 0))]
                + _pre_weight_specs(weights, layer)
                + [pl.BlockSpec((None, 1) + SDIMS, lambda s: (layer, 0, 0, 0, 0)),
                   pl.BlockSpec((None, 1) + PDIMS, lambda s: (layer, 0, 0, 0))]
                + _mix_const_specs(cnt, layer))
    args = [x_rows, *weights, s0, pp, cnt, *mix_consts]
    aliases = {}
    for out_idx, buf in zip((2, 3), prev or ()):
        in_specs.append(pl.BlockSpec(memory_space=pl.ANY))
        aliases[len(args)] = out_idx
        args.append(buf)
    return pl.pallas_call(
        functools.partial(_premix_kernel, tiles_per_seq=tiles_per_seq),
        grid=(n + 1,),
        in_specs=in_specs,
        input_output_aliases=aliases,
        out_specs=[pl.BlockSpec((tm, D_MODEL), lambda s: (jnp.minimum(s, n - 1), 0)),
                   pl.BlockSpec((tm, D_MODEL), lambda s: (mix_tile(s), 0)),
                   pl.BlockSpec((None, 1) + SDIMS, lambda s: (layer, mix_tile(s) // tiles_per_seq, 0, 0, 0)),
                   pl.BlockSpec((None, 1) + PDIMS, lambda s: (layer, mix_tile(s) // tiles_per_seq, 0, 0))],
        out_shape=[jax.ShapeDtypeStruct((m, D_MODEL), F32), jax.ShapeDtypeStruct((m, D_MODEL), BF16),
                   jax.ShapeDtypeStruct((DEPTH, n_seq) + SDIMS, F32),
                   jax.ShapeDtypeStruct((DEPTH, n_seq) + PDIMS, F32)],
        scratch_shapes=[pltpu.VMEM((2, tm, HG_WIDTH), F32)] * N_PROJ
        + [pltpu.VMEM((tm, D_MODEL), F32), pltpu.VMEM(SDIMS, F32),
           pltpu.VMEM((16 + tm, POOL_WIDTH), F32), pltpu.VMEM((tm, D_MODEL), F32)],
        compiler_params=pltpu.CompilerParams(dimension_semantics=("arbitrary",),
                                             vmem_limit_bytes=VMEM_LIMIT),
        name="premix",
    )(*args)


def _post_kernel(x1_ref, mix_ref, wo_ref, g2_ref, w2i_ref, w2o_ref, gf_ref, *rest, final, n_cast):
    cast_in = rest[:n_cast]
    y_ref = rest[n_cast]
    cast_out = rest[n_cast + 1:2 * n_cast + 1]
    acc_ref = rest[-1]
    x2 = x1_ref[...] + _dot(mix_ref[...].astype(BF16), wo_ref[...])
    _run(_ffn_half_steps(x2, g2_ref[...], w2i_ref, w2o_ref, acc_ref))
    x3 = x2 + 0.5 * acc_ref[...]
    y_ref[...] = _rmsnorm(x3, gf_ref[...]) if final else x3
    for src, dst in zip(cast_in, cast_out):
        dst[...] = src[...].astype(BF16)


def _cast_block_rows(rows, n_steps):
    bf16_rows = 2 * SUBLANES
    per_step = rows // n_steps
    hold = 1
    while per_step * hold % bf16_rows:
        hold *= 2
    assert rows % (per_step * hold) == 0 and n_steps % hold == 0
    return per_step * hold, hold


def _post_call(x1_rows, mix_rows, weights, gf, layer, final, tm, cast=None):
    m = x1_rows.shape[0]
    n_steps = m // tm
    row = lambda w: pl.BlockSpec((tm, w), lambda i: (i, 0))
    cast_arrays, cast_layer = cast or ((), None)
    cast_in_specs, cast_out_specs, cast_shapes = [], [], []
    for a in cast_arrays:
        _, rows, cols = a.shape
        blk, hold = _cast_block_rows(rows, n_steps)
        cast_in_specs.append(pl.BlockSpec((None, blk, cols), lambda i, hold=hold: (cast_layer, i // hold, 0)))
        cast_out_specs.append(pl.BlockSpec((blk, cols), lambda i, hold=hold: (i // hold, 0)))
        cast_shapes.append(jax.ShapeDtypeStruct((rows, cols), BF16))
    out = pl.pallas_call(
        functools.partial(_post_kernel, final=final, n_cast=len(cast_arrays)),
        grid=(n_steps,),
        in_specs=[row(D_MODEL), row(D_MODEL)] + [_param_spec(w, layer) for w in weights]
        + [_fixed_spec((1, D_MODEL))] + cast_in_specs,
        out_specs=[row(D_MODEL)] + cast_out_specs,
        out_shape=[jax.ShapeDtypeStruct((m, D_MODEL), F32)] + cast_shapes,
        scratch_shapes=[pltpu.VMEM((tm, D_MODEL), F32)],
        compiler_params=pltpu.CompilerParams(dimension_semantics=("arbitrary",),
                                             vmem_limit_bytes=VMEM_LIMIT),
        name="post",
    )(x1_rows, mix_rows, *weights, gf, *cast_arrays)
    return out[0], tuple(out[1:])


def _split3_rows(e):
    e1 = e.astype(BF16).astype(F32)
    r1 = e - e1
    e2 = r1.astype(BF16).astype(F32)
    e3 = r1 - e2
    row = lax.broadcasted_iota(jnp.int32, (SUBLANES, e.shape[1]), 0)
    return jnp.where(row == 0, e1, jnp.where(row == 1, e2, jnp.where(row == 2, e3, 0.0))).astype(BF16)


def _dot_tn(a, b):
    return lax.dot_general(a, b, (((0,), (0,)), ((), ())), preferred_element_type=F32)


def _sample_mix_kernel(q_ref, k_ref, lf_ref, v_ref, gt_ref, zp_ref, s0_ref, pp_ref, cnt_ref, hgn_ref, pw_ref,
                       ps_ref, *rest, nseq, group, t_real):
    mix_ref, sout_ref, pout_ref, ubuf, dbuf, mixf = rest[-6:]
    D = HG_D
    T = SUBLANES
    row_w = lax.broadcasted_iota(jnp.int32, (T, HG_WIDTH), 0)
    row_d = lax.broadcasted_iota(jnp.int32, (T, D), 0)
    ones = jnp.ones((T, D), BF16)

    def pad_rows(a):
        return jnp.concatenate([a, jnp.zeros((T - t_real, a.shape[1]), F32)], axis=0)

    def group_step(g):
        prepared = []
        for j in range(group):
            n = g * group + j
            rows = pl.ds(pl.multiple_of(n * T, T), T)
            b = pad_rows(lf_ref[n])
            for shift in (1, 2, 4):
                b = b + jnp.where(row_w >= shift, pltpu.roll(b, shift, 0), 0.0)
            q_all, k_all, v_all, gt_all = pad_rows(q_ref[n]), pad_rows(k_ref[n]), pad_rows(v_ref[n]), pad_rows(gt_ref[n])
            for h in range(HG_HEADS):
                hs = slice(h * D, (h + 1) * D)
                bh = b[:, hs]
                qh = q_all[:, hs]
                kh = k_all[:, hs]
                vh = v_all[:, hs]
                blast = bh[T - 1:T, :]
                qe = (qh * jnp.exp2(bh)).astype(BF16)
                kd = (kh * jnp.exp2(blast - bh)).astype(BF16)
                dec = _split3_rows(jnp.exp2(blast))
                o_intra = jnp.zeros((T, D), F32)
                for s in range(t_real):
                    p = qh * kh[s:s + 1, :] * jnp.exp2(jnp.where(row_d >= s, bh - bh[s:s + 1, :], NEG_BIG))
                    o_intra = o_intra + jnp.sum(p, axis=-1, keepdims=True) * vh[s:s + 1, :]
                prepared.append((n, rows, h, hs, qe, kd, vh.astype(BF16), dec, o_intra, gt_all[:, hs]))
        products = []
        for n, rows, h, hs, qe, kd, vb, dec, o_intra, gate in prepared:
            st = s0_ref[n, h]
            products.append((st, _dot(qe, st.astype(BF16)), _dot_tn(kd, vb), _dot_tn(dec, ones)))
        for (n, rows, h, hs, qe, kd, vb, dec, o_intra, gate), (st, o_inter, upd, dec_rows) in zip(prepared, products):
            sout_ref[n, h] = st * dec_rows + upd
            o = o_inter + o_intra
            on = o * lax.rsqrt(jnp.mean(o * o, axis=-1, keepdims=True) + EPS) * hgn_ref[:, hs]
            mixf[rows, hs] = on * gate
        for j in range(group):
            n = g * group + j
            rows = pl.ds(pl.multiple_of(n * T, T), T)
            ubuf[j, 1:1 + POOL_STATE, :] = pp_ref[n]
            ubuf[j, 16:16 + T, :] = pad_rows(zp_ref[n])
            for gi, w in enumerate(POOL_WINDOWS):
                ls = slice(gi * POOL_GC, (gi + 1) * POOL_GC)
                cur = ubuf[j, 16:16 + T, ls]
                acc = cur
                for jj in range(1, w):
                    acc = acc + ubuf[j, 16 - jj:16 - jj + T, ls]
                dbuf[rows, ls] = acc / cnt_ref[:, ls] - cur
            pout_ref[n] = ubuf[j, t_real + 1:t_real + 1 + POOL_STATE, :]

    lax.fori_loop(0, nseq // group, lambda g, _: (group_step(g), 0)[1], 0)
    for gi in range(len(POOL_WINDOWS)):
        ls = slice(gi * POOL_GC, (gi + 1) * POOL_GC)
        mixf[:, HG_WIDTH + gi * POOL_GC:HG_WIDTH + (gi + 1) * POOL_GC] = (
            _dot(dbuf[:, ls].astype(BF16), pw_ref[gi]) * ps_ref[:, ls])

    def write_rows(n, _):
        mix_ref[n] = mixf[pl.ds(pl.multiple_of(n * T, T), t_real), :].astype(mix_ref.dtype)
        return 0

    lax.fori_loop(0, nseq, write_rows, 0)


def _sample_mix_call(proj, s0, pp, cnt, mix_consts, prev, *, layer, nseq, group, mix_dtype):
    n_seq_total, t_real, _ = proj[0].shape
    assert n_seq_total % nseq == 0 and nseq % group == 0 and t_real <= SUBLANES
    r = nseq * SUBLANES
    rows = lambda w: pl.BlockSpec((nseq, t_real, w), lambda bi: (bi, 0, 0))
    in_specs = ([rows(HG_WIDTH)] * N_PROJ
                + [pl.BlockSpec((None, nseq) + SDIMS, lambda bi: (layer, bi, 0, 0, 0)),
                   pl.BlockSpec((None, nseq) + PDIMS, lambda bi: (layer, bi, 0, 0))]
                + _mix_const_specs(cnt, layer))
    args = [*proj, s0, pp, cnt, *mix_consts]
    aliases = {}
    for out_idx, buf in zip((1, 2), prev):
        if buf is not None:
            in_specs.append(pl.BlockSpec(memory_space=pl.ANY))
            aliases[len(args)] = out_idx
            args.append(buf)
    return pl.pallas_call(
        functools.partial(_sample_mix_kernel, nseq=nseq, group=group, t_real=t_real),
        grid=(n_seq_total // nseq,),
        in_specs=in_specs,
        input_output_aliases=aliases,
        out_specs=[pl.BlockSpec((nseq, t_real, D_MODEL), lambda bi: (bi, 0, 0)),
                   pl.BlockSpec((None, nseq) + SDIMS, lambda bi: (layer, bi, 0, 0, 0)),
                   pl.BlockSpec((None, nseq) + PDIMS, lambda bi: (layer, bi, 0, 0))],
        out_shape=[jax.ShapeDtypeStruct((n_seq_total, t_real, D_MODEL), mix_dtype),
                   jax.ShapeDtypeStruct((DEPTH, n_seq_total) + SDIMS, F32),
                   jax.ShapeDtypeStruct((DEPTH, n_seq_total) + PDIMS, F32)],
        scratch_shapes=[pltpu.VMEM((group, 16 + SUBLANES, POOL_WIDTH), F32), pltpu.VMEM((r, POOL_WIDTH), F32),
                        pltpu.VMEM((r, D_MODEL), F32)],
        compiler_params=pltpu.CompilerParams(dimension_semantics=("arbitrary",),
                                             vmem_limit_bytes=VMEM_LIMIT),
        name="mix_sample",
    )(*args)


def _mix_kernel(q_ref, k_ref, lf_ref, v_ref, gt_ref, zp_ref, s0_ref, pp_ref, cnt_ref, hgn_ref, pw_ref, ps_ref,
                *rest, tt, chunk, nseq, t_real):
    mix_ref, sout_ref, pout_ref, st_scr, ubuf, mixf = rest[-6:]
    proj = (q_ref, k_ref, lf_ref, v_ref, gt_ref, zp_ref)
    consts = (cnt_ref, hgn_ref, pw_ref, ps_ref)

    def seq_step(n):
        _load_state(s0_ref, pp_ref, n, st_scr, ubuf)
        base = n * tt if isinstance(n, int) else pl.multiple_of(n * tt, SUBLANES)
        _run(_mix_sequence_tile_steps(proj, consts, mixf, st_scr, ubuf, base=base, tt=tt, chunk=chunk,
                                      t_real=t_real, unrolled=False))
        _store_state(sout_ref, pout_ref, n, st_scr, ubuf, t_real)

    if nseq == 1:
        seq_step(0)
    else:
        lax.fori_loop(0, nseq, lambda n, _: (seq_step(n), 0)[1], 0)
    mix_ref[...] = mixf[...].astype(mix_ref.dtype)


def _mix_call(proj, s0, s0_layer, pp, pp_layer, cnt, mix_consts, prev, *, layer, row_offset, n_seq_total,
              tt, chunk, nseq, t_real, mix_dtype):
    m = proj[0].shape[0]
    assert n_seq_total % nseq == 0
    r = nseq * tt
    assert row_offset % r == 0
    blk0 = row_offset // r
    rows = lambda w: pl.BlockSpec((r, w), lambda bi: (blk0 + bi, 0))
    in_specs = ([rows(HG_WIDTH)] * N_PROJ
                + [pl.BlockSpec((None, nseq) + SDIMS, lambda bi: (s0_layer, bi, 0, 0, 0)),
                   pl.BlockSpec((None, nseq) + PDIMS, lambda bi: (pp_layer, bi, 0, 0))]
                + _mix_const_specs(cnt, layer))
    args = [*proj, s0, pp, cnt, *mix_consts]
    aliases = {}
    for out_idx, buf in enumerate(prev):
        if buf is not None:
            in_specs.append(pl.BlockSpec(memory_space=pl.ANY))
            aliases[len(args)] = out_idx
            args.append(buf)
    kern = functools.partial(_mix_kernel, tt=tt, chunk=chunk, nseq=nseq, t_real=t_real)
    return pl.pallas_call(
        kern,
        grid=(n_seq_total // nseq,),
        in_specs=in_specs,
        input_output_aliases=aliases,
        out_specs=[pl.BlockSpec((r, D_MODEL), lambda bi: (blk0 + bi, 0)),
                   pl.BlockSpec((None, nseq) + SDIMS, lambda bi: (layer, bi, 0, 0, 0)),
                   pl.BlockSpec((None, nseq) + PDIMS, lambda bi: (layer, bi, 0, 0))],
        out_shape=[jax.ShapeDtypeStruct((m, D_MODEL), mix_dtype),
                   jax.ShapeDtypeStruct((DEPTH, n_seq_total) + SDIMS, F32),
                   jax.ShapeDtypeStruct((DEPTH, n_seq_total) + PDIMS, F32)],
        scratch_shapes=[pltpu.VMEM(SDIMS, F32), pltpu.VMEM((16 + tt, POOL_WIDTH), F32),
                        pltpu.VMEM((r, D_MODEL), F32)],
        compiler_params=pltpu.CompilerParams(dimension_semantics=("arbitrary",),
                                             vmem_limit_bytes=VMEM_LIMIT),
        name="mix",
    )(*args)


def _lower_bound_params(lb_logits):
    p = jax.nn.softmax(lb_logits.astype(F32), axis=0)
    cs = jnp.cumsum(p, axis=0)
    lb = cs - cs[0:1]
    rows = jnp.stack([1.0 - lb, jnp.log(lb), jnp.log1p(-lb)], axis=1)
    return jnp.pad(rows, ((0, 0), (0, SUBLANES - 3), (0, 0)))


def kernel(x_prompt, x_sample, state_hgrn, state_pool, meta, lb_logits, norm_ffn1, w_ffn1_in, w_ffn1_out,
           norm_mix, w_in, hg_norm, pool_w, pool_scale, w_out, norm_ffn2, w_ffn2_in, w_ffn2_out, norm_final):
    bp, sp, _ = x_prompt.shape
    bs, ss, _ = x_sample.shape
    n_prompt = bp * sp
    n_sample = bs * ss
    n_side = n_sample + N_META
    xp = x_prompt.reshape(n_prompt, D_MODEL)
    x_side = jnp.concatenate([x_sample.reshape(n_sample, D_MODEL), meta.astype(F32)], axis=0)

    gain3 = lambda a: a.reshape(DEPTH, 1, -1)
    big_f32 = (w_ffn1_in, w_ffn1_out, w_in, w_out, w_ffn2_in, w_ffn2_out)
    big = tuple(w[0].astype(BF16) for w in big_f32)
    lbp = _lower_bound_params(lb_logits)
    mix_consts = (hg_norm.reshape(DEPTH, 1, HG_WIDTH), pool_w.astype(BF16), gain3(pool_scale))
    gf = norm_final.reshape(1, D_MODEL)
    wmax = jnp.repeat(jnp.asarray(POOL_WINDOWS, F32), POOL_GC)[None, :]
    cnt_meta = jnp.minimum(wmax, jnp.arange(1, N_META + 1, dtype=F32)[:, None])
    zero_s = jnp.zeros((1, 1) + SDIMS, F32)
    zero_p = jnp.zeros((1, 1) + PDIMS, F32)

    st_m = st_p = st_s = (None, None)
    for l in range(DEPTH):
        last = l == DEPTH - 1
        w1i, w1o, wi, wo, w2i, w2o = big
        pre_w = (gain3(norm_ffn1), w1i, w1o, gain3(norm_mix), wi, lbp)
        post_w = (wo, gain3(norm_ffn2), w2i, w2o)
        x1_side, *proj = _pre_call(x_side, pre_w, l, n_side)
        mix_meta, *st_m = _mix_call(proj, zero_s, 0, zero_p, 0, cnt_meta, mix_consts, (None, *st_m),
                                    layer=l, row_offset=n_sample, n_seq_total=1,
                                    tt=N_META, chunk=SUBLANES, nseq=1, t_real=N_META, mix_dtype=F32)
        x1, mix, *st_p = _premix_call(xp, pre_w, st_m[0], st_m[1], wmax, mix_consts,
                                      None if l == 0 else st_p, layer=l, seq_len=sp)
        proj_sample = [a[:n_sample].reshape(bs, ss, HG_WIDTH) for a in proj]
        mix_sample, *st_s = _sample_mix_call(proj_sample, state_hgrn, state_pool, wmax, mix_consts, st_s,
                                             layer=l, nseq=SAMPLE_SEQS, group=SAMPLE_GROUP, mix_dtype=F32)
        mix_side = jnp.concatenate([mix_sample.reshape(n_sample, D_MODEL), mix_meta[n_sample:]], axis=0)
        xp, big_next = _post_call(x1, mix, post_w, gf, l, last, ROW_TILE, None if last else (big_f32, l + 1))
        x_side, _ = _post_call(x1_side, mix_side, post_w, gf, l, last, n_side)
        big = big_next

    y_prompt = xp.reshape(bp, sp, D_MODEL)
    y_sample = x_side[:n_sample].reshape(bs, ss, D_MODEL)
    return (y_prompt, y_sample, st_p[0], st_p[1], st_s[0], st_s[1])
```
